```python
import jax, jax.numpy as jnp
from jax import lax
import numpy as np

D_MODEL = 2048
BATCH = 2
SEQ = 16384
DEPTH = 1

HEAD_DIM = 64
A_Q_HEADS = (3 * D_MODEL) // (4 * HEAD_DIM)
A_KV_HEADS = A_Q_HEADS // 4
B_HEADS = D_MODEL // (4 * HEAD_DIM)
B_CONFIGS = ((128, 1), (512, 4), (2048, 16))
N_B_GROUPS = len(B_CONFIGS)
A_Q_W = A_Q_HEADS * HEAD_DIM
A_KV_W = A_KV_HEADS * HEAD_DIM
B_W = B_HEADS * HEAD_DIM
MIX_WIDTH = A_Q_W + B_W
IN_WIDTH = A_Q_W + 2 * A_KV_W + 3 * N_B_GROUPS * B_W
GRID_W = 64
ROPE_THETA = 10000.0
Q_BLOCK = 128
N_EXPERTS = 32
TOP_K = 4
D_FF = D_MODEL
SWIGLU_LIMIT = 7.0
SWIGLU_ALPHA = 1.702
MOE_BLOCK = 512
NORM_EPS = 1e-5
QK_EPS = 1e-6

kernel_name = "hymba_gqa_axial_dilated_alibi_moe_encoder"


def rms_norm(x, w, eps):
    xf = x.astype(jnp.float32)
    y = xf * lax.rsqrt(jnp.mean(xf * xf, axis=-1, keepdims=True) + eps)
    return (y * w.astype(jnp.float32)).astype(x.dtype)


def axial_rope_tables(seq_len):
    rows = seq_len // GRID_W
    r, c = jnp.meshgrid(jnp.arange(rows), jnp.arange(GRID_W), indexing="ij")
    axis_dim = HEAD_DIM // 2
    inv = ROPE_THETA ** (-jnp.arange(0, axis_dim, 2, dtype=jnp.float32) / axis_dim)
    ang_r = r.reshape(-1).astype(jnp.float32)[:, None] * inv[None, :]
    ang_c = c.reshape(-1).astype(jnp.float32)[:, None] * inv[None, :]
    ang = jnp.concatenate([ang_r, ang_c], axis=-1)
    return jnp.cos(ang), jnp.sin(ang)


def rotate(xp, cos, sin):
    x1, x2 = jnp.split(xp, 2, axis=-1)
    return jnp.concatenate([x1 * cos - x2 * sin, x2 * cos + x1 * sin], axis=-1)


def apply_axial_rope(x, cos, sin):
    quarter = HEAD_DIM // 4
    cr, cc = cos[:, None, :quarter], cos[:, None, quarter:]
    sr, sc = sin[:, None, :quarter], sin[:, None, quarter:]
    xf = x.astype(jnp.float32)
    half = HEAD_DIM // 2
    out = jnp.concatenate([rotate(xf[..., :half], cr, sr), rotate(xf[..., half:], cc, sc)], axis=-1)
    return out.astype(x.dtype)


def gqa_axial_attention(q, k, v, q_norm_w, k_norm_w):
    B, S = q.shape[0], q.shape[1]
    q = rms_norm(q, q_norm_w, QK_EPS)
    k = rms_norm(k, k_norm_w, QK_EPS)
    cos, sin = axial_rope_tables(S)
    q = apply_axial_rope(q, cos, sin)
    k = apply_axial_rope(k, cos, sin)
    G = A_Q_HEADS // A_KV_HEADS
    scale = HEAD_DIM ** -0.5
    n_blk = S // Q_BLOCK
    qb_all = jnp.moveaxis(q.reshape(B, n_blk, Q_BLOCK, A_KV_HEADS, G, HEAD_DIM), 1, 0)

    def block(qb):
        s = jnp.einsum("bqhgd,bkhd->bhgqk", qb, k, preferred_element_type=jnp.float32) * scale
        p = jax.nn.softmax(s, axis=-1)
        return jnp.einsum("bhgqk,bkhd->bqhgd", p.astype(v.dtype), v)

    o = lax.map(block, qb_all)
    return jnp.moveaxis(o, 0, 1).reshape(B, S, A_Q_HEADS * HEAD_DIM)


def dilated_window_attention(q, k, v, window, dilation, slopes):
    B, S, H, D = q.shape
    radius = window // (2 * dilation)
    offs = jnp.arange(-radius, radius + 1)
    alibi = -slopes[:, None] * (dilation * jnp.abs(offs)).astype(jnp.float32)[None, :]
    scale = D ** -0.5
    n_blk = S // Q_BLOCK
    qb_all = jnp.moveaxis(q.reshape(B, n_blk, Q_BLOCK, H, D), 1, 0)

    def block(args):
        i, qb = args
        pos = i * Q_BLOCK + jnp.arange(Q_BLOCK)
        idx = pos[:, None] + dilation * offs[None, :]
        valid = (idx >= 0) & (idx < S)
        idx = jnp.clip(idx, 0, S - 1)
        kg = jnp.take(k, idx, axis=1)
        vg = jnp.take(v, idx, axis=1)
        s = jnp.einsum("bqhd,bqkhd->bhqk", qb, kg, preferred_element_type=jnp.float32) * scale
        s = s + alibi[None, :, None, :]
        s = jnp.where(valid[None, None], s, -jnp.inf)
        lse = jax.nn.logsumexp(s, axis=-1)
        p = jnp.exp(s - lse[..., None])
        o = jnp.einsum("bhqk,bqkhd->bqhd", p.astype(v.dtype), vg)
        return o, lse

    o, lse = lax.map(block, (jnp.arange(n_blk), qb_all))
    o = jnp.moveaxis(o, 0, 1).reshape(B, S, H, D)
    lse = jnp.transpose(lse, (1, 0, 3, 2)).reshape(B, S, H)
    return o, lse


def alibi_slopes(n_heads):
    return 2.0 ** (-8.0 * (jnp.arange(n_heads, dtype=jnp.float32) + 1.0) / n_heads)


def moe_ffn(xn, w_router, b_router, w_gate_up, b_gate_up, w_down, b_down):
    N, D = xn.shape
    logits = jnp.einsum("nd,de->ne", xn, w_router, preferred_element_type=jnp.float32) + b_router.astype(jnp.float32)
    top_vals, top_idx = lax.top_k(logits, TOP_K)
    gates = jax.nn.softmax(top_vals, axis=-1)
    A = N * TOP_K
    flat_e = top_idx.reshape(-1)
    flat_tok = jnp.repeat(jnp.arange(N, dtype=jnp.int32), TOP_K)
    flat_g = gates.reshape(-1)
    order = jnp.argsort(flat_e)
    sorted_e = flat_e[order]
    counts = jnp.bincount(flat_e, length=N_EXPERTS)
    start = jnp.cumsum(counts) - counts
    rank = jnp.arange(A, dtype=jnp.int32) - start[sorted_e]
    padded = (counts + MOE_BLOCK - 1) // MOE_BLOCK * MOE_BLOCK
    pend = jnp.cumsum(padded)
    pstart = pend - padded
    dest = pstart[sorted_e] + rank
    n_blocks = -(-A // MOE_BLOCK) + N_EXPERTS
    cap = n_blocks * MOE_BLOCK
    tok_buf = jnp.full((cap,), N, jnp.int32).at[dest].set(flat_tok[order])
    gate_buf = jnp.zeros((cap,), jnp.float32).at[dest].set(flat_g[order])
    blk_e = jnp.minimum(jnp.searchsorted(pend, jnp.arange(n_blocks) * MOE_BLOCK, side="right"), N_EXPERTS - 1)
    x_pad = jnp.concatenate([xn, jnp.zeros((1, D), xn.dtype)], axis=0)

    def expert_block(args):
        e, toks, g = args
        xb = x_pad[toks]
        hgu = xb @ w_gate_up[e] + b_gate_up[e]
        gate, up = jnp.split(hgu, 2, axis=-1)
        gate = jnp.minimum(gate, SWIGLU_LIMIT)
        up = jnp.clip(up, -SWIGLU_LIMIT, SWIGLU_LIMIT)
        act = (up + 1.0) * gate * jax.nn.sigmoid(SWIGLU_ALPHA * gate)
        y = act @ w_down[e] + b_down[e]
        return y.astype(jnp.float32) * g[:, None]

    y = lax.map(expert_block, (blk_e, tok_buf.reshape(n_blocks, MOE_BLOCK), gate_buf.reshape(n_blocks, MOE_BLOCK)))
    out = jax.ops.segment_sum(y.reshape(cap, D), tok_buf, num_segments=N + 1)[:N]
    return out.astype(xn.dtype)


def setup_inputs(seed: int = 0) -> dict:
    key = jax.random.key(seed)
    ks = jax.random.split(key, 16)
    f32 = jnp.float32
    L, D, E, F = DEPTH, D_MODEL, N_EXPERTS, D_FF

    def nrm(k, shape, scale):
        return jax.random.normal(k, shape, f32) * scale

    return {
        "x": jax.random.normal(ks[0], (BATCH, SEQ, D), f32),
        "norm_mix_w": 1.0 + nrm(ks[1], (L, D), 0.02),
        "w_in": nrm(ks[2], (L, D, IN_WIDTH), D ** -0.5),
        "q_norm_w": 1.0 + nrm(ks[3], (L, HEAD_DIM), 0.02),
        "k_norm_w": 1.0 + nrm(ks[4], (L, HEAD_DIM), 0.02),
        "out_norm_w": 1.0 + nrm(ks[5], (L, MIX_WIDTH), 0.02),
        "w_out": nrm(ks[6], (L, MIX_WIDTH, D), MIX_WIDTH ** -0.5),
        "norm_moe_w": 1.0 + nrm(ks[7], (L, D), 0.02),
        "w_router": nrm(ks[8], (L, D, E), D ** -0.5),
        "b_router": nrm(ks[9], (L, E), 0.01),
        "w_gate_up": nrm(ks[10], (L, E, D, 2 * F), D ** -0.5),
        "b_gate_up": nrm(ks[11], (L, E, 2 * F), 0.02),
        "w_down": nrm(ks[12], (L, E, F, D), F ** -0.5),
        "b_down": nrm(ks[13], (L, E, D), 0.02),
        "final_norm_w": 1.0 + nrm(ks[14], (D,), 0.02),
    }


def reference(x, norm_mix_w, w_in, q_norm_w, k_norm_w, out_norm_w, w_out, norm_moe_w,
              w_router, b_router, w_gate_up, b_gate_up, w_down, b_down, final_norm_w):
    B, S, D = x.shape
    slopes = alibi_slopes(B_HEADS)
    for l in range(DEPTH):
        h = rms_norm(x, norm_mix_w[l], NORM_EPS)
        p = jnp.einsum("bsd,de->bse", h, w_in[l])
        qa = p[..., :A_Q_W].reshape(B, S, A_Q_HEADS, HEAD_DIM)
        ka = p[..., A_Q_W:A_Q_W + A_KV_W].reshape(B, S, A_KV_HEADS, HEAD_DIM)
        va = p[..., A_Q_W + A_KV_W:A_Q_W + 2 * A_KV_W].reshape(B, S, A_KV_HEADS, HEAD_DIM)
        o_a = gqa_axial_attention(qa, ka, va, q_norm_w[l], k_norm_w[l])

        base = A_Q_W + 2 * A_KV_W
        outs, lses = [], []
        for g, (window, dilation) in enumerate(B_CONFIGS):
            off = base + g * 3 * B_W
            qb = p[..., off:off + B_W].reshape(B, S, B_HEADS, HEAD_DIM)
            kb = p[..., off + B_W:off + 2 * B_W].reshape(B, S, B_HEADS, HEAD_DIM)
            vb = p[..., off + 2 * B_W:off + 3 * B_W].reshape(B, S, B_HEADS, HEAD_DIM)
            o_g, lse_g = dilated_window_attention(qb, kb, vb, window, dilation, slopes)
            outs.append(o_g)
            lses.append(lse_g)
        wts = jax.nn.softmax(jnp.stack(lses, axis=0), axis=0)
        o_b = jnp.sum(wts[..., None] * jnp.stack(outs, axis=0).astype(jnp.float32), axis=0)
        o_b = o_b.astype(x.dtype).reshape(B, S, B_W)

        merged = jnp.concatenate([
            rms_norm(o_a, out_norm_w[l, :A_Q_W], NORM_EPS),
            rms_norm(o_b, out_norm_w[l, A_Q_W:], NORM_EPS),
        ], axis=-1)
        x = x + jnp.einsum("bse,ed->bsd", merged, w_out[l])

        hm = rms_norm(x, norm_moe_w[l], NORM_EPS).reshape(B * S, D)
        y = moe_ffn(hm, w_router[l], b_router[l], w_gate_up[l], b_gate_up[l], w_down[l], b_down[l])
        x = x + y.reshape(B, S, D)
    return rms_norm(x, final_norm_w, NORM_EPS)
```

```python
import functools

import jax
import jax.numpy as jnp
from jax import lax
from jax.experimental import pallas as pl
from jax.experimental.pallas import tpu as pltpu

F32 = jnp.float32
BF16 = jnp.bfloat16
I32 = jnp.int32

HEAD_DIM = 64
GQA_RATIO = 4
B_CONFIGS = ((128, 1), (512, 4), (2048, 16))
GRID_W = 64
ROPE_THETA = 10000.0
N_EXPERTS = 32
TOP_K = 4
MOE_BLOCK = 512
SWIGLU_LIMIT = 7.0
SWIGLU_ALPHA = 1.702
NORM_EPS = 1e-5
QK_EPS = 1e-6
LOG2E = 1.4426950408889634
NEG_BIG = -1e30
V7X_VMEM_LIMIT = 56 * 1024 * 1024


def _params(*sem):
    return pltpu.CompilerParams(dimension_semantics=sem, vmem_limit_bytes=V7X_VMEM_LIMIT)


def _prenorm_body(x_ref, w_ref, o_ref):
    x = x_ref[...]
    ms = jnp.mean(x * x, axis=-1, keepdims=True)
    o_ref[...] = (x * lax.rsqrt(ms + NORM_EPS) * w_ref[...]).astype(o_ref.dtype)


def _prenorm(x2, w, tm=512):
    n, d = x2.shape
    return pl.pallas_call(
        _prenorm_body,
        grid=(n // tm,),
        in_specs=[pl.BlockSpec((tm, d), lambda i: (i, 0)), pl.BlockSpec((1, d), lambda i: (0, 0))],
        out_specs=pl.BlockSpec((tm, d), lambda i: (i, 0)),
        out_shape=jax.ShapeDtypeStruct((n, d), BF16),
        compiler_params=_params("parallel"),
        name="prenorm",
    )(x2, w.reshape(1, d))


def _aproj_body(n_q, n_kv, h_ref, wqv_ref, wk_ref, qnw_ref, knw_ref, cos_ref, sin_ref,
                kc_ref, ksm_ref, ksp_ref, bd_ref, qT_ref, k_ref, vT_ref, pt_ref):
    h = h_ref[0]
    pt_ref[...] = lax.dot_general(wqv_ref[...], h, (((1,), (1,)), ((), ())),
                                  preferred_element_type=F32)
    q4 = HEAD_DIM // 4
    cr, cc = cos_ref[0:q4, :], cos_ref[q4:2 * q4, :]
    sr, sc = sin_ref[0:q4, :], sin_ref[q4:2 * q4, :]
    qnw = qnw_ref[...]

    def q_head(hh, carry):
        y = pt_ref[pl.ds(pl.multiple_of(hh * HEAD_DIM, HEAD_DIM), HEAD_DIM), :]
        ms = jnp.mean(y * y, axis=0, keepdims=True)
        y = y * lax.rsqrt(ms + QK_EPS) * qnw
        a1, a2, b1, b2 = y[0:q4], y[q4:2 * q4], y[2 * q4:3 * q4], y[3 * q4:]
        out = jnp.concatenate([a1 * cr - a2 * sr, a2 * cr + a1 * sr,
                               b1 * cc - b2 * sc, b2 * cc + b1 * sc], axis=0)
        qT_ref[0, hh] = (out * (HEAD_DIM ** -0.5 * LOG2E)).astype(BF16)
        return carry

    lax.fori_loop(0, n_q, q_head, 0)
    for g in range(n_kv):
        lo = (n_q + g) * HEAD_DIM
        vT_ref[0, g] = pt_ref[lo:lo + HEAD_DIM, :].astype(BF16)

    kn = jnp.dot(h, wk_ref[...], preferred_element_type=F32)
    ms = jnp.dot(kn * kn, bd_ref[...], precision=lax.Precision.HIGHEST,
                 preferred_element_type=F32)
    kn = kn * lax.rsqrt(ms + QK_EPS) * knw_ref[...]
    for c in range(n_kv // 2):
        y = kn[:, c * 128:(c + 1) * 128]
        out = (y * kc_ref[...] + pltpu.roll(y, 128 - q4, 1) * ksm_ref[...]
               + pltpu.roll(y, q4, 1) * ksp_ref[...])
        k_ref[0, 2 * c] = out[:, :HEAD_DIM].astype(BF16)
        k_ref[0, 2 * c + 1] = out[:, HEAD_DIM:].astype(BF16)


def _rope_tables(seq_len):
    rows = seq_len // GRID_W
    r, c = jnp.meshgrid(jnp.arange(rows), jnp.arange(GRID_W), indexing="ij")
    axis_dim = HEAD_DIM // 2
    inv = ROPE_THETA ** (-jnp.arange(0, axis_dim, 2, dtype=F32) / axis_dim)
    ang_r = r.reshape(-1).astype(F32)[:, None] * inv[None, :]
    ang_c = c.reshape(-1).astype(F32)[:, None] * inv[None, :]
    ang = jnp.concatenate([ang_r, ang_c], axis=-1)
    return jnp.cos(ang), jnp.sin(ang)


def _aproj(h3, w_q, w_k, w_v, q_norm_w, k_norm_w, ts=512):
    b, s, d = h3.shape
    n_q, n_kv = w_q.shape[1] // HEAD_DIM, w_k.shape[1] // HEAD_DIM
    q4 = HEAD_DIM // 4
    cos, sin = _rope_tables(s)
    zeros = jnp.zeros_like(sin[:, :q4])
    c64 = jnp.concatenate([cos[:, :q4], cos[:, :q4], cos[:, q4:], cos[:, q4:]], axis=1)
    sm64 = jnp.concatenate([-sin[:, :q4], zeros, -sin[:, q4:], zeros], axis=1)
    sp64 = jnp.concatenate([zeros, sin[:, :q4], zeros, sin[:, q4:]], axis=1)
    kc, ksm, ksp = (jnp.concatenate([t, t], axis=1) for t in (c64, sm64, sp64))
    head_of = jnp.arange(n_kv * HEAD_DIM) // HEAD_DIM
    bd = (head_of[:, None] == head_of[None, :]).astype(F32) / HEAD_DIM
    wqv_t = jnp.concatenate([w_q, w_v], axis=1).T.astype(BF16)
    rows = wqv_t.shape[0]
    full = lambda shape: pl.BlockSpec(shape, lambda bi, i: (0,) * len(shape))
    return pl.pallas_call(
        functools.partial(_aproj_body, n_q, n_kv),
        grid=(b, s // ts),
        in_specs=[
            pl.BlockSpec((1, ts, d), lambda bi, i: (bi, i, 0)),
            full((rows, d)), full((d, n_kv * HEAD_DIM)),
            full((HEAD_DIM, 1)), full((1, n_kv * HEAD_DIM)),
            pl.BlockSpec((2 * q4, ts), lambda bi, i: (0, i)),
            pl.BlockSpec((2 * q4, ts), lambda bi, i: (0, i)),
            pl.BlockSpec((ts, 128), lambda bi, i: (i, 0)),
            pl.BlockSpec((ts, 128), lambda bi, i: (i, 0)),
            pl.BlockSpec((ts, 128), lambda bi, i: (i, 0)),
            full((n_kv * HEAD_DIM, n_kv * HEAD_DIM)),
        ],
        out_specs=[
            pl.BlockSpec((1, n_q, HEAD_DIM, ts), lambda bi, i: (bi, 0, 0, i)),
            pl.BlockSpec((1, n_kv, ts, HEAD_DIM), lambda bi, i: (bi, 0, i, 0)),
            pl.BlockSpec((1, n_kv, HEAD_DIM, ts), lambda bi, i: (bi, 0, 0, i)),
        ],
        out_shape=[
            jax.ShapeDtypeStruct((b, n_q, HEAD_DIM, s), BF16),
            jax.ShapeDtypeStruct((b, n_kv, s, HEAD_DIM), BF16),
            jax.ShapeDtypeStruct((b, n_kv, HEAD_DIM, s), BF16),
        ],
        scratch_shapes=[pltpu.VMEM((rows, ts), F32)],
        compiler_params=_params("parallel", "parallel"),
        name="aproj",
    )(h3, wqv_t, w_k.astype(BF16), q_norm_w.reshape(HEAD_DIM, 1),
      jnp.tile(k_norm_w, n_kv).reshape(1, n_kv * HEAD_DIM), cos.T, sin.T, kc, ksm, ksp, bd)


def _attn_a_body(tk, qT_ref, k_ref, vT_ref, o_ref, o_scr):
    s_len = k_ref.shape[2]
    tq = qT_ref.shape[3]
    for hh in range(GQA_RATIO):
        q = qT_ref[0, hh]

        def kv_step(j, carry):
            m, l, acc = carry
            off = pl.multiple_of(j * tk, tk)
            k = k_ref[0, 0, pl.ds(off, tk), :]
            v = vT_ref[0, 0, :, pl.ds(off, tk)]
            s = jnp.dot(k, q, preferred_element_type=F32)
            m_new = jnp.maximum(m, jnp.max(s, axis=0, keepdims=True))
            p = jnp.exp2(s - m_new)
            alpha = jnp.exp2(m - m_new)
            l = alpha * l + jnp.sum(p, axis=0, keepdims=True)
            acc = alpha * acc + jnp.dot(v, p.astype(BF16), preferred_element_type=F32)
            return m_new, l, acc

        init = (jnp.full((1, tq), NEG_BIG, F32), jnp.zeros((1, tq), F32),
                jnp.zeros((HEAD_DIM, tq), F32))
        m, l, acc = lax.fori_loop(0, s_len // tk, kv_step, init)
        o_scr[hh * HEAD_DIM:(hh + 1) * HEAD_DIM, :] = acc / l
    o_ref[0] = o_scr[...].T.astype(o_ref.dtype)


def _attn_a(qT, k, vT, tq=256, tk=256):
    b, n_q, _, s = qT.shape
    n_kv = k.shape[1]
    return pl.pallas_call(
        functools.partial(_attn_a_body, tk),
        grid=(b, n_kv, s // tq),
        in_specs=[
            pl.BlockSpec((1, GQA_RATIO, HEAD_DIM, tq), lambda bi, g, i: (bi, g, 0, i)),
            pl.BlockSpec((1, 1, s, HEAD_DIM), lambda bi, g, i: (bi, g, 0, 0)),
            pl.BlockSpec((1, 1, HEAD_DIM, s), lambda bi, g, i: (bi, g, 0, 0)),
        ],
        out_specs=pl.BlockSpec((1, tq, GQA_RATIO * HEAD_DIM), lambda bi, g, i: (bi, i, g)),
        out_shape=jax.ShapeDtypeStruct((b, s, n_q * HEAD_DIM), BF16),
        scratch_shapes=[pltpu.VMEM((GQA_RATIO * HEAD_DIM, tq), F32)],
        compiler_params=_params("parallel", "parallel", "parallel"),
        name="attn_a",
    )(qT, k, vT)


def _matmul_body(a_ref, b_ref, o_ref):
    o_ref[...] = jnp.dot(a_ref[...], b_ref[...], preferred_element_type=F32).astype(o_ref.dtype)


def _matmul(a, w, tm=512, tn=1536):
    n, d = a.shape
    e = w.shape[1]
    return pl.pallas_call(
        _matmul_body,
        grid=(e // tn, n // tm),
        in_specs=[pl.BlockSpec((tm, d), lambda j, i: (i, 0)), pl.BlockSpec((d, tn), lambda j, i: (0, j))],
        out_specs=pl.BlockSpec((tm, tn), lambda j, i: (i, j)),
        out_shape=jax.ShapeDtypeStruct((n, e), BF16),
        compiler_params=_params("parallel", "parallel"),
        name="bproj",
    )(a, w)


def _attn_b_body(dilation, radius, q_ref, kp_ref, kc_ref, kn_ref, vp_ref, vc_ref, vn_ref,
                 o_ref, lse_ref):
    tu = q_ref.shape[1]
    n_pairs = q_ref.shape[2] // 128
    i = pl.program_id(2)
    n_u = pl.num_programs(2) * tu
    row = lax.broadcasted_iota(I32, (tu, 2 * tu), 0)
    col = lax.broadcasted_iota(I32, (tu, 2 * tu), 1)
    j = col - radius - row
    key = i * tu - radius + col
    valid = (jnp.abs(j) <= radius) & (key >= 0) & (key < n_u)
    dist = (dilation * jnp.abs(j)).astype(F32)
    pen = jnp.where(valid, 0.0, NEG_BIG)
    first = lax.broadcasted_iota(I32, (1, 128), 1) < HEAD_DIM
    scale = HEAD_DIM ** -0.5
    n_heads = 2 * n_pairs
    for hp in range(n_pairs):
        lanes = slice(hp * 128, (hp + 1) * 128)
        q2 = q_ref[0, :, lanes]
        kw = jnp.concatenate([kp_ref[0, tu - radius:, lanes], kc_ref[0, :, lanes],
                              kn_ref[0, :tu - radius, lanes]], axis=0)
        vw = jnp.concatenate([vp_ref[0, tu - radius:, lanes], vc_ref[0, :, lanes],
                              vn_ref[0, :tu - radius, lanes]], axis=0)
        o2 = jnp.zeros((tu, 128), F32)
        lse2 = jnp.zeros((tu, 128), F32)
        for sub in range(2):
            sel = first if sub == 0 else jnp.logical_not(first)
            slope = 2.0 ** (-8.0 * (2 * hp + sub + 1) / n_heads)
            qm = jnp.where(sel, q2, jnp.zeros_like(q2))
            s = lax.dot_general(qm, kw, (((1,), (1,)), ((), ())), preferred_element_type=F32)
            s = s * scale + (pen - slope * dist)
            m = jnp.max(s, axis=1, keepdims=True)
            p = jnp.exp(s - m)
            l = jnp.sum(p, axis=1, keepdims=True)
            vm = jnp.where(sel, vw, jnp.zeros_like(vw))
            o2 = o2 + jnp.dot(p.astype(BF16), vm, preferred_element_type=F32) / l
            lse2 = jnp.where(sel, m + jnp.log(l), lse2)
        o_ref[0, :, lanes] = o2.astype(o_ref.dtype)
        lse_ref[0, :, lanes] = lse2


def _attn_b(pb, group, window, dilation, width, tu=128):
    b, s, e = pb.shape
    u = s // dilation
    radius = window // (2 * dilation)
    assert radius < tu and u % tu == 0
    pbd = pb.reshape(b, u, dilation * e)
    cpb = e // width
    base = group * 3
    n_t = u // tu

    def spec(which, shift):
        def index(bi, r, i):
            return (bi, jnp.clip(i + shift, 0, n_t - 1), r * cpb + base + which)
        return pl.BlockSpec((1, tu, width), index)

    out_spec = pl.BlockSpec((1, tu, width), lambda bi, r, i: (bi, i, r))
    o, lse = pl.pallas_call(
        functools.partial(_attn_b_body, dilation, radius),
        grid=(b, dilation, n_t),
        in_specs=[spec(0, 0), spec(1, -1), spec(1, 0), spec(1, 1), spec(2, -1), spec(2, 0), spec(2, 1)],
        out_specs=[out_spec, out_spec],
        out_shape=[jax.ShapeDtypeStruct((b, u, dilation * width), BF16),
                   jax.ShapeDtypeStruct((b, u, dilation * width), F32)],
        compiler_params=_params("parallel", "parallel", "parallel"),
        name=f"attn_b{group}",
    )(pbd, pbd, pbd, pbd, pbd, pbd, pbd)
    return o.reshape(b, s, width), lse.reshape(b, s, width)


def _rms(x, w):
    ms = jnp.mean(x * x, axis=-1, keepdims=True)
    return x * lax.rsqrt(ms + NORM_EPS) * w


def _outproj_body(n_groups, *refs):
    oa_ref = refs[0]
    ob_refs = refs[1:1 + n_groups]
    lse_refs = refs[1 + n_groups:1 + 2 * n_groups]
    (x_ref, wo_a_ref, wo_b_ref, nwa_ref, nwb_ref, nm_ref, wr_ref, br_ref,
     x1_ref, hm_ref, lg_ref) = refs[1 + 2 * n_groups:]
    lses = [r[...] for r in lse_refs]
    mx = functools.reduce(jnp.maximum, lses)
    ws = [jnp.exp(v - mx) for v in lses]
    den = functools.reduce(lambda a, c: a + c, ws)
    ob = functools.reduce(lambda a, c: a + c, [w * r[...].astype(F32) for w, r in zip(ws, ob_refs)]) / den
    na = _rms(oa_ref[...].astype(F32), nwa_ref[...]).astype(BF16)
    nb = _rms(ob, nwb_ref[...]).astype(BF16)
    x1 = (x_ref[...] + jnp.dot(na, wo_a_ref[...], preferred_element_type=F32)
          + jnp.dot(nb, wo_b_ref[...], preferred_element_type=F32))
    x1_ref[...] = x1
    hm = _rms(x1, nm_ref[...])
    hm_ref[...] = hm.astype(BF16)
    lg_ref[...] = jnp.dot(hm, wr_ref[...], precision=lax.Precision.HIGHEST,
                          preferred_element_type=F32) + br_ref[...]


def _outproj(o_a, o_bs, lses, x2, w_out, out_norm_w, norm_moe_w, w_router, b_router, tm=256):
    n, d = x2.shape
    wa, wb = o_a.shape[1], o_bs[0].shape[1]
    n_e = w_router.shape[1]
    row = lambda width: pl.BlockSpec((tm, width), lambda i: (i, 0))
    full = lambda shape: pl.BlockSpec(shape, lambda i: (0,) * len(shape))
    n_groups = len(o_bs)
    return pl.pallas_call(
        functools.partial(_outproj_body, n_groups),
        grid=(n // tm,),
        in_specs=[row(wa)] + [row(wb)] * (2 * n_groups) + [
            row(d), full((wa, d)), full((wb, d)), full((1, wa)), full((1, wb)), full((1, d)),
            full((d, n_e)), full((1, n_e))],
        out_specs=[row(d), row(d), row(n_e)],
        out_shape=[jax.ShapeDtypeStruct((n, d), F32), jax.ShapeDtypeStruct((n, d), BF16),
                   jax.ShapeDtypeStruct((n, n_e), F32)],
        compiler_params=_params("parallel"),
        name="outproj",
    )(o_a, *o_bs, *lses, x2, w_out[:wa].astype(BF16), w_out[wa:].astype(BF16),
      out_norm_w[:wa].reshape(1, wa), out_norm_w[wa:].reshape(1, wb), norm_moe_w.reshape(1, d),
      w_router, b_router.reshape(1, n_e))


def _router_body(lg_ref, idx_ref, gate_ref, rank_ref, cnt_ref, carry_ref):
    i = pl.program_id(0)
    tm, n_e = lg_ref.shape

    @pl.when(i == 0)
    def _():
        carry_ref[...] = jnp.zeros_like(carry_ref)

    lane = lax.broadcasted_iota(I32, (tm, n_e), 1)
    work = lg_ref[...]
    vals, idxs = [], []
    for _ in range(TOP_K):
        mx = jnp.max(work, axis=1, keepdims=True)
        ix = jnp.min(jnp.where(work == mx, lane, n_e), axis=1, keepdims=True)
        vals.append(mx)
        idxs.append(ix)
        work = jnp.where(lane == ix, -jnp.inf, work)
    es = [jnp.exp(v - vals[0]) for v in vals]
    den = functools.reduce(lambda a, c: a + c, es)
    chosen = functools.reduce(lambda a, c: a + c, [(lane == ix).astype(F32) for ix in idxs])
    r = lax.broadcasted_iota(I32, (tm, tm), 0)
    c = lax.broadcasted_iota(I32, (tm, tm), 1)
    before = jnp.where(c < r, 1.0, 0.0).astype(BF16)
    prefix = jnp.dot(before, chosen.astype(BF16), preferred_element_type=F32) + carry_ref[...]
    for k in range(TOP_K):
        idx_ref[:, k:k + 1] = idxs[k]
        gate_ref[:, k:k + 1] = es[k] / den
        rank_ref[:, k:k + 1] = jnp.sum(jnp.where(lane == idxs[k], prefix, 0.0), axis=1,
                                       keepdims=True).astype(I32)
    carry_ref[...] += jnp.sum(chosen, axis=0, keepdims=True)
    cnt_ref[...] = carry_ref[...].astype(I32)


def _router(logits, tm=256):
    n, n_e = logits.shape
    row = pl.BlockSpec((tm, TOP_K), lambda i: (i, 0))
    return pl.pallas_call(
        _router_body,
        grid=(n // tm,),
        in_specs=[pl.BlockSpec((tm, n_e), lambda i: (i, 0))],
        out_specs=[row, row, row, pl.BlockSpec((1, n_e), lambda i: (0, 0))],
        out_shape=[jax.ShapeDtypeStruct((n, TOP_K), I32), jax.ShapeDtypeStruct((n, TOP_K), F32),
                   jax.ShapeDtypeStruct((n, TOP_K), I32), jax.ShapeDtypeStruct((1, n_e), I32)],
        scratch_shapes=[pltpu.VMEM((1, n_e), F32)],
        compiler_params=_params("arbitrary"),
        name="router",
    )(logits)


def _row_gather_body(rows_per_step, idx_ref, src_ref, dst_ref, sem):
    base = pl.program_id(0) * rows_per_step

    def issue(r, carry):
        pltpu.make_async_copy(src_ref.at[idx_ref[0, 0, r]], dst_ref.at[base + r], sem).start()
        return carry

    lax.fori_loop(0, rows_per_step, issue, 0)
    pltpu.make_async_copy(src_ref.at[pl.ds(0, rows_per_step)],
                          dst_ref.at[pl.ds(base, rows_per_step)], sem).wait()


def _row_gather(src3, idx, rows_per_step=1024):
    n_out = idx.shape[0]
    steps = n_out // rows_per_step
    return pl.pallas_call(
        functools.partial(_row_gather_body, rows_per_step),
        grid=(steps,),
        in_specs=[pl.BlockSpec((1, 1, rows_per_step), lambda i: (i, 0, 0), memory_space=pltpu.SMEM),
                  pl.BlockSpec(memory_space=pl.ANY)],
        out_specs=pl.BlockSpec(memory_space=pl.ANY),
        out_shape=jax.ShapeDtypeStruct((n_out,) + src3.shape[1:], src3.dtype),
        scratch_shapes=[pltpu.SemaphoreType.DMA(())],
        compiler_params=_params("arbitrary"),
        name="dispatch",
    )(idx.reshape(steps, 1, rows_per_step), src3)


def _ffn_body(blk_e_ref, n_used_ref, x_ref, wg_ref, wu_ref, bg_ref, bu_ref, wd_ref, bd_ref, g_ref,
              y_ref, acc_ref):
    i, f = pl.program_id(0), pl.program_id(1)
    n_f = pl.num_programs(1)
    used = i < n_used_ref[0]

    @pl.when(used)
    def _():
        x = x_ref[...]
        gate = jnp.dot(x, wg_ref[0], preferred_element_type=F32) + bg_ref[0]
        up = jnp.dot(x, wu_ref[0], preferred_element_type=F32) + bu_ref[0]
        gate = jnp.minimum(gate, SWIGLU_LIMIT)
        up = jnp.clip(up, -SWIGLU_LIMIT, SWIGLU_LIMIT)
        act = (up + 1.0) * gate * jax.nn.sigmoid(SWIGLU_ALPHA * gate)
        part = jnp.dot(act.astype(BF16), wd_ref[0], preferred_element_type=F32)

        @pl.when(f == 0)
        def _():
            acc_ref[...] = part

        @pl.when(f > 0)
        def _():
            acc_ref[...] += part

        @pl.when(f == n_f - 1)
        def _():
            y_ref[...] = ((acc_ref[...] + bd_ref[0]) * g_ref[...]).astype(y_ref.dtype)

    @pl.when(jnp.logical_not(used) & (f == n_f - 1))
    def _():
        y_ref[...] = jnp.zeros_like(y_ref)


def _expert_ffn(xs, blk_e, n_used, w_gu, b_gu, w_down, b_down, gate_buf, tf=512):
    cap, d = xs.shape
    n_e, _, two_f = w_gu.shape
    ff = two_f // 2
    n_f = ff // tf
    n_blocks = cap // MOE_BLOCK

    def fidx(i, f, nu):
        return jnp.where(i < nu[0], f, n_f - 1)

    grid_spec = pltpu.PrefetchScalarGridSpec(
        num_scalar_prefetch=2,
        grid=(n_blocks, n_f),
        in_specs=[
            pl.BlockSpec((MOE_BLOCK, d), lambda i, f, be, nu: (jnp.minimum(i, nu[0] - 1), 0)),
            pl.BlockSpec((1, d, tf), lambda i, f, be, nu: (be[i], 0, fidx(i, f, nu))),
            pl.BlockSpec((1, d, tf), lambda i, f, be, nu: (be[i], 0, n_f + fidx(i, f, nu))),
            pl.BlockSpec((1, 1, tf), lambda i, f, be, nu: (be[i], 0, fidx(i, f, nu))),
            pl.BlockSpec((1, 1, tf), lambda i, f, be, nu: (be[i], 0, n_f + fidx(i, f, nu))),
            pl.BlockSpec((1, tf, d), lambda i, f, be, nu: (be[i], fidx(i, f, nu), 0)),
            pl.BlockSpec((1, 1, d), lambda i, f, be, nu: (be[i], 0, 0)),
            pl.BlockSpec((MOE_BLOCK, 1), lambda i, f, be, nu: (i, 0)),
        ],
        out_specs=pl.BlockSpec((MOE_BLOCK, d), lambda i, f, be, nu: (i, 0)),
        scratch_shapes=[pltpu.VMEM((MOE_BLOCK, d), F32)],
    )
    return pl.pallas_call(
        _ffn_body,
        grid_spec=grid_spec,
        out_shape=jax.ShapeDtypeStruct((cap, d), F32),
        compiler_params=_params("arbitrary", "arbitrary"),
        name="expert_ffn",
    )(blk_e, n_used, xs, w_gu, w_gu, b_gu.reshape(n_e, 1, two_f), b_gu.reshape(n_e, 1, two_f),
      w_down, b_down.reshape(n_e, 1, d), gate_buf.reshape(cap, 1))


def _combine_body(tm, dest_ref, y_ref, x1_ref, w_ref, o_ref, buf_ref, sem):
    def issue(t, carry):
        for k in range(TOP_K):
            pltpu.make_async_copy(y_ref.at[dest_ref[0, 0, t * TOP_K + k]], buf_ref.at[k, t], sem).start()
        return carry

    lax.fori_loop(0, tm, issue, 0)
    for k in range(TOP_K):
        pltpu.make_async_copy(y_ref.at[pl.ds(0, tm)], buf_ref.at[k], sem).wait()
    x = x1_ref[...]
    for k in range(TOP_K):
        x = x + buf_ref[k]
    ms = jnp.sum(jnp.sum(x * x, axis=2, keepdims=True), axis=1, keepdims=True) / (x.shape[1] * x.shape[2])
    o_ref[...] = x * lax.rsqrt(ms + NORM_EPS) * w_ref[...]


def _combine(y3, dest, x13, w, tm=256):
    n, sl, _ = x13.shape
    steps = n // tm
    return pl.pallas_call(
        functools.partial(_combine_body, tm),
        grid=(steps,),
        in_specs=[pl.BlockSpec((1, 1, tm * TOP_K), lambda i: (i, 0, 0), memory_space=pltpu.SMEM),
                  pl.BlockSpec(memory_space=pl.ANY),
                  pl.BlockSpec((tm, sl, 128), lambda i: (i, 0, 0)),
                  pl.BlockSpec((1, sl, 128), lambda i: (0, 0, 0))],
        out_specs=pl.BlockSpec((tm, sl, 128), lambda i: (i, 0, 0)),
        out_shape=jax.ShapeDtypeStruct((n, sl, 128), F32),
        scratch_shapes=[pltpu.VMEM((TOP_K, tm, sl, 128), F32), pltpu.SemaphoreType.DMA(())],
        compiler_params=_params("arbitrary"),
        name="combine",
    )(dest.reshape(steps, 1, tm * TOP_K), y3, x13, w.reshape(1, sl, 128))


def _layer(x, norm_mix_w, w_in, q_norm_w, k_norm_w, out_norm_w, w_out, norm_moe_w,
           w_router, b_router, w_gate_up, b_gate_up, w_down, b_down):
    b, s, d = x.shape
    n = b * s
    x2 = x.reshape(n, d)
    n_groups = len(B_CONFIGS)
    b_w = d // 4
    a_q_w = d - b_w
    a_kv_w = a_q_w // GQA_RATIO

    h = _prenorm(x2, norm_mix_w)
    qT, k, vT = _aproj(h.reshape(b, s, d), w_in[:, :a_q_w], w_in[:, a_q_w:a_q_w + a_kv_w],
                       w_in[:, a_q_w + a_kv_w:a_q_w + 2 * a_kv_w], q_norm_w, k_norm_w)
    o_a = _attn_a(qT, k, vT).reshape(n, a_q_w)

    pb = _matmul(h, w_in[:, a_q_w + 2 * a_kv_w:].astype(BF16)).reshape(b, s, 3 * n_groups * b_w)
    o_bs, lses = [], []
    for g, (window, dilation) in enumerate(B_CONFIGS):
        o_g, lse_g = _attn_b(pb, g, window, dilation, b_w)
        o_bs.append(o_g.reshape(n, b_w))
        lses.append(lse_g.reshape(n, b_w))

    x1, hm, logits = _outproj(o_a, o_bs, lses, x2, w_out, out_norm_w, norm_moe_w, w_router, b_router)

    top_idx, gates, rank, counts = _router(logits)
    counts = counts.reshape(N_EXPERTS)
    padded = (counts + MOE_BLOCK - 1) // MOE_BLOCK * MOE_BLOCK
    pend = jnp.cumsum(padded)
    pstart = pend - padded
    dest = pstart[top_idx] + rank
    n_blocks = -(-(n * TOP_K) // MOE_BLOCK) + N_EXPERTS
    cap = n_blocks * MOE_BLOCK
    flat_tok = jnp.repeat(jnp.arange(n, dtype=I32), TOP_K)
    tok_buf = jnp.zeros((cap,), I32).at[dest.reshape(-1)].set(flat_tok)
    gate_buf = jnp.zeros((cap,), F32).at[dest.reshape(-1)].set(gates.reshape(-1))
    blk_e = jnp.minimum(jnp.searchsorted(pend, jnp.arange(n_blocks, dtype=I32) * MOE_BLOCK, side="right"),
                        N_EXPERTS - 1).astype(I32)
    n_used = (pend[-1] // MOE_BLOCK).astype(I32).reshape(1)

    xs = _row_gather(hm.reshape(n, d // 128, 128), tok_buf).reshape(cap, d)
    y = _expert_ffn(xs, blk_e, n_used, w_gate_up.astype(BF16), b_gate_up, w_down.astype(BF16), b_down, gate_buf)
    return y, dest, x1


def kernel(x, norm_mix_w, w_in, q_norm_w, k_norm_w, out_norm_w, w_out, norm_moe_w, w_router, b_router,
           w_gate_up, b_gate_up, w_down, b_down, final_norm_w):
    b, s, d = x.shape
    n = b * s
    depth = w_in.shape[0]
    for l in range(depth):
        y, dest, x1 = _layer(x, norm_mix_w[l], w_in[l], q_norm_w[l], k_norm_w[l], out_norm_w[l], w_out[l],
                             norm_moe_w[l], w_router[l], b_router[l], w_gate_up[l], b_gate_up[l],
                             w_down[l], b_down[l])
        last = l == depth - 1
        w_fin = final_norm_w if last else jnp.ones((d,), F32)
        assert last, "only the final layer's combine is fused with a norm"
        x = _combine(y.reshape(-1, d // 128, 128), dest, x1.reshape(n, d // 128, 128), w_fin).reshape(b, s, d)
    return x
```

```python
import functools

import jax
import jax.numpy as jnp
from jax import lax
from jax.experimental import pallas as pl
from jax.experimental.pallas import tpu as pltpu

F32 = jnp.float32
BF16 = jnp.bfloat16
I32 = jnp.int32

HEAD_DIM = 64
GQA_RATIO = 4
B_CONFIGS = ((128, 1), (512, 4), (2048, 16))
GRID_W = 64
ROPE_THETA = 10000.0
N_EXPERTS = 32
TOP_K = 4
MOE_BLOCK = 512
SWIGLU_LIMIT = 7.0
SWIGLU_ALPHA = 1.702
NORM_EPS = 1e-5
QK_EPS = 1e-6
LOG2E = 1.4426950408889634
NEG_BIG = -1e30
V_PAD_ROWS = 16
V7X_VMEM_LIMIT = 56 * 1024 * 1024


def _params(*sem):
    return pltpu.CompilerParams(dimension_semantics=sem, vmem_limit_bytes=V7X_VMEM_LIMIT)


def _prenorm_body(x_ref, w_ref, o_ref):
    x = x_ref[...]
    ms = jnp.mean(x * x, axis=-1, keepdims=True)
    o_ref[...] = (x * lax.rsqrt(ms + NORM_EPS) * w_ref[...]).astype(o_ref.dtype)


def _prenorm(x2, w, tm=512):
    n, d = x2.shape
    return pl.pallas_call(
        _prenorm_body,
        grid=(n // tm,),
        in_specs=[pl.BlockSpec((tm, d), lambda i: (i, 0)), pl.BlockSpec((1, d), lambda i: (0, 0))],
        out_specs=pl.BlockSpec((tm, d), lambda i: (i, 0)),
        out_shape=jax.ShapeDtypeStruct((n, d), BF16),
        compiler_params=_params("parallel"),
        name="prenorm",
    )(x2, w.reshape(1, d))


def _aproj_body(n_q, n_kv, h_ref, wqv_ref, wk_ref, qnw_ref, knw_ref, cos_ref, sin_ref,
                kc_ref, ksm_ref, ksp_ref, bd_ref, qT_ref, k_ref, vT_ref, pt_ref):
    h = h_ref[0]
    pt_ref[...] = lax.dot_general(wqv_ref[...], h, (((1,), (1,)), ((), ())),
                                  preferred_element_type=F32)
    q4 = HEAD_DIM // 4
    cr, cc = cos_ref[0:q4, :], cos_ref[q4:2 * q4, :]
    sr, sc = sin_ref[0:q4, :], sin_ref[q4:2 * q4, :]
    qnw = qnw_ref[...]

    def q_head(hh, carry):
        y = pt_ref[pl.ds(pl.multiple_of(hh * HEAD_DIM, HEAD_DIM), HEAD_DIM), :]
        ms = jnp.mean(y * y, axis=0, keepdims=True)
        y = y * lax.rsqrt(ms + QK_EPS) * qnw
        a1, a2, b1, b2 = y[0:q4], y[q4:2 * q4], y[2 * q4:3 * q4], y[3 * q4:]
        out = jnp.concatenate([a1 * cr - a2 * sr, a2 * cr + a1 * sr,
                               b1 * cc - b2 * sc, b2 * cc + b1 * sc], axis=0)
        qT_ref[0, hh] = (out * (HEAD_DIM ** -0.5 * LOG2E)).astype(BF16)
        return carry

    lax.fori_loop(0, n_q, q_head, 0)
    ones_row = (lax.broadcasted_iota(I32, (V_PAD_ROWS, pt_ref.shape[1]), 0) == 0).astype(BF16)
    for g in range(n_kv):
        lo = (n_q + g) * HEAD_DIM
        vT_ref[0, g, :HEAD_DIM] = pt_ref[lo:lo + HEAD_DIM, :].astype(BF16)
        vT_ref[0, g, HEAD_DIM:] = ones_row

    kn = jnp.dot(h, wk_ref[...], preferred_element_type=F32)
    ms = jnp.dot(kn * kn, bd_ref[...], precision=lax.Precision.HIGHEST,
                 preferred_element_type=F32)
    kn = kn * lax.rsqrt(ms + QK_EPS) * knw_ref[...]
    for c in range(n_kv // 2):
        y = kn[:, c * 128:(c + 1) * 128]
        out = (y * kc_ref[...] + pltpu.roll(y, 128 - q4, 1) * ksm_ref[...]
               + pltpu.roll(y, q4, 1) * ksp_ref[...])
        k_ref[0, 2 * c] = out[:, :HEAD_DIM].astype(BF16)
        k_ref[0, 2 * c + 1] = out[:, HEAD_DIM:].astype(BF16)


def _rope_tables(seq_len):
    rows = seq_len // GRID_W
    r, c = jnp.meshgrid(jnp.arange(rows), jnp.arange(GRID_W), indexing="ij")
    axis_dim = HEAD_DIM // 2
    inv = ROPE_THETA ** (-jnp.arange(0, axis_dim, 2, dtype=F32) / axis_dim)
    ang_r = r.reshape(-1).astype(F32)[:, None] * inv[None, :]
    ang_c = c.reshape(-1).astype(F32)[:, None] * inv[None, :]
    ang = jnp.concatenate([ang_r, ang_c], axis=-1)
    return jnp.cos(ang), jnp.sin(ang)


def _aproj(h3, w_q, w_k, w_v, q_norm_w, k_norm_w, ts=512):
    b, s, d = h3.shape
    n_q, n_kv = w_q.shape[1] // HEAD_DIM, w_k.shape[1] // HEAD_DIM
    q4 = HEAD_DIM // 4
    cos, sin = _rope_tables(s)
    zeros = jnp.zeros_like(sin[:, :q4])
    c64 = jnp.concatenate([cos[:, :q4], cos[:, :q4], cos[:, q4:], cos[:, q4:]], axis=1)
    sm64 = jnp.concatenate([-sin[:, :q4], zeros, -sin[:, q4:], zeros], axis=1)
    sp64 = jnp.concatenate([zeros, sin[:, :q4], zeros, sin[:, q4:]], axis=1)
    kc, ksm, ksp = (jnp.concatenate([t, t], axis=1) for t in (c64, sm64, sp64))
    head_of = jnp.arange(n_kv * HEAD_DIM) // HEAD_DIM
    bd = (head_of[:, None] == head_of[None, :]).astype(F32) / HEAD_DIM
    wqv_t = jnp.concatenate([w_q, w_v], axis=1).T.astype(BF16)
    rows = wqv_t.shape[0]
    full = lambda shape: pl.BlockSpec(shape, lambda bi, i: (0,) * len(shape))
    return pl.pallas_call(
        functools.partial(_aproj_body, n_q, n_kv),
        grid=(b, s // ts),
        in_specs=[
            pl.BlockSpec((1, ts, d), lambda bi, i: (bi, i, 0)),
            full((rows, d)), full((d, n_kv * HEAD_DIM)),
            full((HEAD_DIM, 1)), full((1, n_kv * HEAD_DIM)),
            pl.BlockSpec((2 * q4, ts), lambda bi, i: (0, i)),
            pl.BlockSpec((2 * q4, ts), lambda bi, i: (0, i)),
            pl.BlockSpec((ts, 128), lambda bi, i: (i, 0)),
            pl.BlockSpec((ts, 128), lambda bi, i: (i, 0)),
            pl.BlockSpec((ts, 128), lambda bi, i: (i, 0)),
            full((n_kv * HEAD_DIM, n_kv * HEAD_DIM)),
        ],
        out_specs=[
            pl.BlockSpec((1, n_q, HEAD_DIM, ts), lambda bi, i: (bi, 0, 0, i)),
            pl.BlockSpec((1, n_kv, ts, HEAD_DIM), lambda bi, i: (bi, 0, i, 0)),
            pl.BlockSpec((1, n_kv, HEAD_DIM + V_PAD_ROWS, ts), lambda bi, i: (bi, 0, 0, i)),
        ],
        out_shape=[
            jax.ShapeDtypeStruct((b, n_q, HEAD_DIM, s), BF16),
            jax.ShapeDtypeStruct((b, n_kv, s, HEAD_DIM), BF16),
            jax.ShapeDtypeStruct((b, n_kv, HEAD_DIM + V_PAD_ROWS, s), BF16),
        ],
        scratch_shapes=[pltpu.VMEM((rows, ts), F32)],
        compiler_params=_params("parallel", "parallel"),
        name="aproj",
    )(h3, wqv_t, w_k.astype(BF16), q_norm_w.reshape(HEAD_DIM, 1),
      jnp.tile(k_norm_w, n_kv).reshape(1, n_kv * HEAD_DIM), cos.T, sin.T, kc, ksm, ksp, bd)


def _attn_a_body(tk, qT_ref, k_ref, vT_ref, o_ref, q_scr, acc_scr, sa_scr, sb_scr, p_scr):
    s_len = k_ref.shape[2]
    tq = qT_ref.shape[3]
    w = GQA_RATIO * tq
    n_steps = s_len // tk
    rc = 32
    for hh in range(GQA_RATIO):
        q_scr[:, hh * tq:(hh + 1) * tq] = qT_ref[0, hh]
    acc_scr[...] = jnp.zeros(acc_scr.shape, F32)

    def scores(j, s_scr):
        off = pl.multiple_of(j * tk, tk)
        s_scr[...] = jnp.dot(k_ref[0, 0, pl.ds(off, tk), :], q_scr[...], preferred_element_type=F32)

    def half_step(j, m, s_scr, nxt_scr):
        scores(jnp.minimum(j + 1, n_steps - 1), nxt_scr)
        off = pl.multiple_of(j * tk, tk)
        v = vT_ref[0, 0, :, pl.ds(off, tk)]
        m8 = jnp.full((8, w), NEG_BIG, F32)
        for c in range(tk // rc):
            sc = s_scr[c * rc:(c + 1) * rc, :]
            m8 = jnp.maximum(m8, jnp.max(sc.reshape(rc // 8, 8, w), axis=0))
        m_new = jnp.maximum(m, jnp.max(m8, axis=0, keepdims=True))
        for c in range(tk // rc):
            sc = s_scr[c * rc:(c + 1) * rc, :]
            p_scr[c * rc:(c + 1) * rc, :] = jnp.exp2(sc - m_new).astype(BF16)
        alpha = jnp.exp2(m - m_new)
        acc_scr[...] = alpha * acc_scr[...] + jnp.dot(v, p_scr[...], preferred_element_type=F32)
        return m_new

    def kv_step(jj, m):
        m = half_step(2 * jj, m, sa_scr, sb_scr)
        return half_step(2 * jj + 1, m, sb_scr, sa_scr)

    scores(0, sa_scr)
    lax.fori_loop(0, n_steps // 2, kv_step, jnp.full((1, w), NEG_BIG, F32))
    o = acc_scr[:HEAD_DIM, :] / acc_scr[HEAD_DIM:HEAD_DIM + 1, :]
    for hh in range(GQA_RATIO):
        o_ref[0, :, hh * HEAD_DIM:(hh + 1) * HEAD_DIM] = o[:, hh * tq:(hh + 1) * tq].T.astype(o_ref.dtype)


def _attn_a(qT, k, vT, tq=256, tk=256):
    b, n_q, _, s = qT.shape
    n_kv = k.shape[1]
    vr = vT.shape[2]
    w = GQA_RATIO * tq
    assert s % (2 * tk) == 0 and s % tq == 0
    return pl.pallas_call(
        functools.partial(_attn_a_body, tk),
        grid=(b, n_kv, s // tq),
        in_specs=[
            pl.BlockSpec((1, GQA_RATIO, HEAD_DIM, tq), lambda bi, g, i: (bi, g, 0, i)),
            pl.BlockSpec((1, 1, s, HEAD_DIM), lambda bi, g, i: (bi, g, 0, 0)),
            pl.BlockSpec((1, 1, vr, s), lambda bi, g, i: (bi, g, 0, 0)),
        ],
        out_specs=pl.BlockSpec((1, tq, GQA_RATIO * HEAD_DIM), lambda bi, g, i: (bi, i, g)),
        out_shape=jax.ShapeDtypeStruct((b, s, n_q * HEAD_DIM), BF16),
        scratch_shapes=[pltpu.VMEM((HEAD_DIM, w), BF16), pltpu.VMEM((vr, w), F32),
                        pltpu.VMEM((tk, w), F32), pltpu.VMEM((tk, w), F32), pltpu.VMEM((tk, w), BF16)],
        compiler_params=_params("parallel", "parallel", "parallel"),
        name="attn_a",
    )(qT, k, vT)


def _matmul_body(a_ref, b_ref, o_ref):
    o_ref[...] = jnp.dot(a_ref[...], b_ref[...], preferred_element_type=F32).astype(o_ref.dtype)


def _matmul(a, w, tm=512, tn=1536):
    n, d = a.shape
    e = w.shape[1]
    return pl.pallas_call(
        _matmul_body,
        grid=(e // tn, n // tm),
        in_specs=[pl.BlockSpec((tm, d), lambda j, i: (i, 0)), pl.BlockSpec((d, tn), lambda j, i: (0, j))],
        out_specs=pl.BlockSpec((tm, tn), lambda j, i: (i, j)),
        out_shape=jax.ShapeDtypeStruct((n, e), BF16),
        compiler_params=_params("parallel", "parallel"),
        name="bproj",
    )(a, w)


def _attn_b_body(dilation, radius, q_ref, kp_ref, kc_ref, kn_ref, vp_ref, vc_ref, vn_ref,
                 o_ref, lse_ref):
    tu = q_ref.shape[1]
    n_pairs = q_ref.shape[2] // 128
    i = pl.program_id(2)
    n_u = pl.num_programs(2) * tu
    row = lax.broadcasted_iota(I32, (tu, 2 * tu), 0)
    col = lax.broadcasted_iota(I32, (tu, 2 * tu), 1)
    j = col - radius - row
    key = i * tu - radius + col
    valid = (jnp.abs(j) <= radius) & (key >= 0) & (key < n_u)
    dist = (dilation * jnp.abs(j)).astype(F32)
    pen = jnp.where(valid, 0.0, NEG_BIG)
    first = lax.broadcasted_iota(I32, (1, 128), 1) < HEAD_DIM
    scale = HEAD_DIM ** -0.5
    n_heads = 2 * n_pairs
    for hp in range(n_pairs):
        lanes = slice(hp * 128, (hp + 1) * 128)
        q2 = q_ref[0, :, lanes]
        kw = jnp.concatenate([kp_ref[0, tu - radius:, lanes], kc_ref[0, :, lanes],
                              kn_ref[0, :tu - radius, lanes]], axis=0)
        vw = jnp.concatenate([vp_ref[0, tu - radius:, lanes], vc_ref[0, :, lanes],
                              vn_ref[0, :tu - radius, lanes]], axis=0)
        o2 = jnp.zeros((tu, 128), F32)
        lse2 = jnp.zeros((tu, 128), F32)
        for sub in range(2):
            sel = first if sub == 0 else jnp.logical_not(first)
            slope = 2.0 ** (-8.0 * (2 * hp + sub + 1) / n_heads)
            qm = jnp.where(sel, q2, jnp.zeros_like(q2))
            s = lax.dot_general(qm, kw, (((1,), (1,)), ((), ())), preferred_element_type=F32)
            s = s * scale + (pen - slope * dist)
            m = jnp.max(s, axis=1, keepdims=True)
            p = jnp.exp(s - m)
            l = jnp.sum(p, axis=1, keepdims=True)
            vm = jnp.where(sel, vw, jnp.zeros_like(vw))
            o2 = o2 + jnp.dot(p.astype(BF16), vm, preferred_element_type=F32) / l
            lse2 = jnp.where(sel, m + jnp.log(l), lse2)
        o_ref[0, :, lanes] = o2.astype(o_ref.dtype)
        lse_ref[0, :, lanes] = lse2


def _attn_b(pb, group, window, dilation, width, tu=128):
    b, s, e = pb.shape
    u = s // dilation
    radius = window // (2 * dilation)
    assert radius < tu and u % tu == 0
    pbd = pb.reshape(b, u, dilation * e)
    cpb = e // width
    base = group * 3
    n_t = u // tu

    def spec(which, shift):
        def index(bi, r, i):
            return (bi, jnp.clip(i + shift, 0, n_t - 1), r * cpb + base + which)
        return pl.BlockSpec((1, tu, width), index)

    out_spec = pl.BlockSpec((1, tu, width), lambda bi, r, i: (bi, i, r))
    o, lse = pl.pallas_call(
        functools.partial(_attn_b_body, dilation, radius),
        grid=(b, dilation, n_t),
        in_specs=[spec(0, 0), spec(1, -1), spec(1, 0), spec(1, 1), spec(2, -1), spec(2, 0), spec(2, 1)],
        out_specs=[out_spec, out_spec],
        out_shape=[jax.ShapeDtypeStruct((b, u, dilation * width), BF16),
                   jax.ShapeDtypeStruct((b, u, dilation * width), F32)],
        compiler_params=_params("parallel", "parallel", "parallel"),
        name=f"attn_b{group}",
    )(pbd, pbd, pbd, pbd, pbd, pbd, pbd)
    return o.reshape(b, s, width), lse.reshape(b, s, width)


def _rms(x, w):
    ms = jnp.mean(x * x, axis=-1, keepdims=True)
    return x * lax.rsqrt(ms + NORM_EPS) * w


def _outproj_body(n_groups, *refs):
    oa_ref = refs[0]
    ob_refs = refs[1:1 + n_groups]
    lse_refs = refs[1 + n_groups:1 + 2 * n_groups]
    (x_ref, wo_a_ref, wo_b_ref, nwa_ref, nwb_ref, nm_ref, wr_ref, br_ref,
     x1_ref, hm_ref, lg_ref) = refs[1 + 2 * n_groups:]
    lses = [r[...] for r in lse_refs]
    mx = functools.reduce(jnp.maximum, lses)
    ws = [jnp.exp(v - mx) for v in lses]
    den = functools.reduce(lambda a, c: a + c, ws)
    ob = functools.reduce(lambda a, c: a + c, [w * r[...].astype(F32) for w, r in zip(ws, ob_refs)]) / den
    na = _rms(oa_ref[...].astype(F32), nwa_ref[...]).astype(BF16)
    nb = _rms(ob, nwb_ref[...]).astype(BF16)
    x1 = (x_ref[...] + jnp.dot(na, wo_a_ref[...], preferred_element_type=F32)
          + jnp.dot(nb, wo_b_ref[...], preferred_element_type=F32))
    x1_ref[...] = x1
    hm = _rms(x1, nm_ref[...])
    hm_ref[...] = hm.astype(BF16)
    lg_ref[...] = jnp.dot(hm, wr_ref[...], precision=lax.Precision.HIGHEST,
                          preferred_element_type=F32) + br_ref[...]


def _outproj(o_a, o_bs, lses, x2, w_out, out_norm_w, norm_moe_w, w_router, b_router, tm=256):
    n, d = x2.shape
    wa, wb = o_a.shape[1], o_bs[0].shape[1]
    n_e = w_router.shape[1]
    row = lambda width: pl.BlockSpec((tm, width), lambda i: (i, 0))
    full = lambda shape: pl.BlockSpec(shape, lambda i: (0,) * len(shape))
    n_groups = len(o_bs)
    return pl.pallas_call(
        functools.partial(_outproj_body, n_groups),
        grid=(n // tm,),
        in_specs=[row(wa)] + [row(wb)] * (2 * n_groups) + [
            row(d), full((wa, d)), full((wb, d)), full((1, wa)), full((1, wb)), full((1, d)),
            full((d, n_e)), full((1, n_e))],
        out_specs=[row(d), row(d), row(n_e)],
        out_shape=[jax.ShapeDtypeStruct((n, d), F32), jax.ShapeDtypeStruct((n, d), BF16),
                   jax.ShapeDtypeStruct((n, n_e), F32)],
        compiler_params=_params("parallel"),
        name="outproj",
    )(o_a, *o_bs, *lses, x2, w_out[:wa].astype(BF16), w_out[wa:].astype(BF16),
      out_norm_w[:wa].reshape(1, wa), out_norm_w[wa:].reshape(1, wb), norm_moe_w.reshape(1, d),
      w_router, b_router.reshape(1, n_e))


def _router_body(lg_ref, idx_ref, gate_ref, rank_ref, cnt_ref, carry_ref):
    i = pl.program_id(0)
    tm, n_e = lg_ref.shape

    @pl.when(i == 0)
    def _():
        carry_ref[...] = jnp.zeros_like(carry_ref)

    lane = lax.broadcasted_iota(I32, (tm, n_e), 1)
    work = lg_ref[...]
    vals, idxs = [], []
    for _ in range(TOP_K):
        mx = jnp.max(work, axis=1, keepdims=True)
        ix = jnp.min(jnp.where(work == mx, lane, n_e), axis=1, keepdims=True)
        vals.append(mx)
        idxs.append(ix)
        work = jnp.where(lane == ix, -jnp.inf, work)
    es = [jnp.exp(v - vals[0]) for v in vals]
    den = functools.reduce(lambda a, c: a + c, es)
    chosen = functools.reduce(lambda a, c: a + c, [(lane == ix).astype(F32) for ix in idxs])
    r = lax.broadcasted_iota(I32, (tm, tm), 0)
    c = lax.broadcasted_iota(I32, (tm, tm), 1)
    before = jnp.where(c < r, 1.0, 0.0).astype(BF16)
    prefix = jnp.dot(before, chosen.astype(BF16), preferred_element_type=F32) + carry_ref[...]
    for k in range(TOP_K):
        idx_ref[:, k:k + 1] = idxs[k]
        gate_ref[:, k:k + 1] = es[k] / den
        rank_ref[:, k:k + 1] = jnp.sum(jnp.where(lane == idxs[k], prefix, 0.0), axis=1,
                                       keepdims=True).astype(I32)
    carry_ref[...] += jnp.sum(chosen, axis=0, keepdims=True)
    cnt_ref[...] = carry_ref[...].astype(I32)


def _router(logits, tm=256):
    n, n_e = logits.shape
    row = pl.BlockSpec((tm, TOP_K), lambda i: (i, 0))
    return pl.pallas_call(
        _router_body,
        grid=(n // tm,),
        in_specs=[pl.BlockSpec((tm, n_e), lambda i: (i, 0))],
        out_specs=[row, row, row, pl.BlockSpec((1, n_e), lambda i: (0, 0))],
        out_shape=[jax.ShapeDtypeStruct((n, TOP_K), I32), jax.ShapeDtypeStruct((n, TOP_K), F32),
                   jax.ShapeDtypeStruct((n, TOP_K), I32), jax.ShapeDtypeStruct((1, n_e), I32)],
        scratch_shapes=[pltpu.VMEM((1, n_e), F32)],
        compiler_params=_params("arbitrary"),
        name="router",
    )(logits)


def _row_gather_body(rows_per_step, idx_ref, src_ref, dst_ref, sem):
    def issue(r, carry):
        pltpu.make_async_copy(src_ref.at[idx_ref[0, 0, r]], dst_ref.at[r], sem).start()
        return carry

    lax.fori_loop(0, rows_per_step, issue, 0)
    pltpu.make_async_copy(src_ref.at[pl.ds(0, rows_per_step)], dst_ref, sem).wait()


def _row_gather(src3, idx, rows_per_step=512):
    n_out = idx.shape[0]
    steps = n_out // rows_per_step
    slab = src3.shape[1:]
    return pl.pallas_call(
        functools.partial(_row_gather_body, rows_per_step),
        grid=(steps,),
        in_specs=[pl.BlockSpec((1, 1, rows_per_step), lambda i: (i, 0, 0), memory_space=pltpu.SMEM),
                  pl.BlockSpec(memory_space=pl.ANY)],
        out_specs=pl.BlockSpec((rows_per_step,) + slab, lambda i: (i, 0, 0)),
        out_shape=jax.ShapeDtypeStruct((n_out,) + slab, src3.dtype),
        scratch_shapes=[pltpu.SemaphoreType.DMA(())],
        compiler_params=_params("arbitrary"),
        name="dispatch",
    )(idx.reshape(steps, 1, rows_per_step), src3)


def _ffn_body(blk_e_ref, n_used_ref, x_ref, wg_ref, wu_ref, bg_ref, bu_ref, wd_ref, bd_ref, g_ref,
              y_ref, acc_ref):
    i, f = pl.program_id(0), pl.program_id(1)
    n_f = pl.num_programs(1)
    used = i < n_used_ref[0]

    @pl.when(used)
    def _():
        x = x_ref[...]
        gate = jnp.dot(x, wg_ref[0], preferred_element_type=F32) + bg_ref[0]
        up = jnp.dot(x, wu_ref[0], preferred_element_type=F32) + bu_ref[0]
        gate = jnp.minimum(gate, SWIGLU_LIMIT)
        up = jnp.clip(up, -SWIGLU_LIMIT, SWIGLU_LIMIT)
        act = (up + 1.0) * gate * jax.nn.sigmoid(SWIGLU_ALPHA * gate)
        part = jnp.dot(act.astype(BF16), wd_ref[0], preferred_element_type=F32)

        @pl.when(f == 0)
        def _():
            acc_ref[...] = part

        @pl.when(f > 0)
        def _():
            acc_ref[...] += part

        @pl.when(f == n_f - 1)
        def _():
            y_ref[...] = ((acc_ref[...] + bd_ref[0]) * g_ref[...]).astype(y_ref.dtype)

    @pl.when(jnp.logical_not(used) & (f == n_f - 1))
    def _():
        y_ref[...] = jnp.zeros_like(y_ref)


def _expert_ffn(xs, blk_e, n_used, w_gu, b_gu, w_down, b_down, gate_buf, tf=512):
    cap, d = xs.shape
    n_e, _, two_f = w_gu.shape
    ff = two_f // 2
    n_f = ff // tf
    n_blocks = cap // MOE_BLOCK

    def fidx(i, f, nu):
        return jnp.where(i < nu[0], f, n_f - 1)

    grid_spec = pltpu.PrefetchScalarGridSpec(
        num_scalar_prefetch=2,
        grid=(n_blocks, n_f),
        in_specs=[
            pl.BlockSpec((MOE_BLOCK, d), lambda i, f, be, nu: (jnp.minimum(i, nu[0] - 1), 0)),
            pl.BlockSpec((1, d, tf), lambda i, f, be, nu: (be[i], 0, fidx(i, f, nu))),
            pl.BlockSpec((1, d, tf), lambda i, f, be, nu: (be[i], 0, n_f + fidx(i, f, nu))),
            pl.BlockSpec((1, 1, tf), lambda i, f, be, nu: (be[i], 0, fidx(i, f, nu))),
            pl.BlockSpec((1, 1, tf), lambda i, f, be, nu: (be[i], 0, n_f + fidx(i, f, nu))),
            pl.BlockSpec((1, tf, d), lambda i, f, be, nu: (be[i], fidx(i, f, nu), 0)),
            pl.BlockSpec((1, 1, d), lambda i, f, be, nu: (be[i], 0, 0)),
            pl.BlockSpec((MOE_BLOCK, 1), lambda i, f, be, nu: (i, 0)),
        ],
        out_specs=pl.BlockSpec((MOE_BLOCK, d), lambda i, f, be, nu: (i, 0)),
        scratch_shapes=[pltpu.VMEM((MOE_BLOCK, d), F32)],
    )
    return pl.pallas_call(
        _ffn_body,
        grid_spec=grid_spec,
        out_shape=jax.ShapeDtypeStruct((cap, d), F32),
        compiler_params=_params("arbitrary", "arbitrary"),
        name="expert_ffn",
    )(blk_e, n_used, xs, w_gu, w_gu, b_gu.reshape(n_e, 1, two_f), b_gu.reshape(n_e, 1, two_f),
      w_down, b_down.reshape(n_e, 1, d), gate_buf.reshape(cap, 1))


def _combine_body(tm, dest_ref, y_ref, x1_ref, w_ref, o_ref, buf_ref, sem):
    def issue(t, carry):
        for k in range(TOP_K):
            pltpu.make_async_copy(y_ref.at[dest_ref[0, 0, t * TOP_K + k]], buf_ref.at[k, t], sem).start()
        return carry

    lax.fori_loop(0, tm, issue, 0)
    for k in range(TOP_K):
        pltpu.make_async_copy(y_ref.at[pl.ds(0, tm)], buf_ref.at[k], sem).wait()
    x = x1_ref[...]
    for k in range(TOP_K):
        x = x + buf_ref[k]
    ms = jnp.sum(jnp.sum(x * x, axis=2, keepdims=True), axis=1, keepdims=True) / (x.shape[1] * x.shape[2])
    o_ref[...] = x * lax.rsqrt(ms + NORM_EPS) * w_ref[...]


def _combine(y3, dest, x13, w, tm=256):
    n, sl, _ = x13.shape
    steps = n // tm
    return pl.pallas_call(
        functools.partial(_combine_body, tm),
        grid=(steps,),
        in_specs=[pl.BlockSpec((1, 1, tm * TOP_K), lambda i: (i, 0, 0), memory_space=pltpu.SMEM),
                  pl.BlockSpec(memory_space=pl.ANY),
                  pl.BlockSpec((tm, sl, 128), lambda i: (i, 0, 0)),
                  pl.BlockSpec((1, sl, 128), lambda i: (0, 0, 0))],
        out_specs=pl.BlockSpec((tm, sl, 128), lambda i: (i, 0, 0)),
        out_shape=jax.ShapeDtypeStruct((n, sl, 128), F32),
        scratch_shapes=[pltpu.VMEM((TOP_K, tm, sl, 128), F32), pltpu.SemaphoreType.DMA(())],
        compiler_params=_params("arbitrary"),
        name="combine",
    )(dest.reshape(steps, 1, tm * TOP_K), y3, x13, w.reshape(1, sl, 128))


def _layer(x, norm_mix_w, w_in, q_norm_w, k_norm_w, out_norm_w, w_out, norm_moe_w,
           w_router, b_router, w_gate_up, b_gate_up, w_down, b_down):
    b, s, d = x.shape
    n = b * s
    x2 = x.reshape(n, d)
    n_groups = len(B_CONFIGS)
    b_w = d // 4
    a_q_w = d - b_w
    a_kv_w = a_q_w // GQA_RATIO

    h = _prenorm(x2, norm_mix_w)
    qT, k, vT = _aproj(h.reshape(b, s, d), w_in[:, :a_q_w], w_in[:, a_q_w:a_q_w + a_kv_w],
                       w_in[:, a_q_w + a_kv_w:a_q_w + 2 * a_kv_w], q_norm_w, k_norm_w)
    o_a = _attn_a(qT, k, vT).reshape(n, a_q_w)

    pb = _matmul(h, w_in[:, a_q_w + 2 * a_kv_w:].astype(BF16)).reshape(b, s, 3 * n_groups * b_w)
    o_bs, lses = [], []
    for g, (window, dilation) in enumerate(B_CONFIGS):
        o_g, lse_g = _attn_b(pb, g, window, dilation, b_w)
        o_bs.append(o_g.reshape(n, b_w))
        lses.append(lse_g.reshape(n, b_w))

    x1, hm, logits = _outproj(o_a, o_bs, lses, x2, w_out, out_norm_w, norm_moe_w, w_router, b_router)

    top_idx, gates, rank, counts = _router(logits)
    counts = counts.reshape(N_EXPERTS)
    padded = (counts + MOE_BLOCK - 1) // MOE_BLOCK * MOE_BLOCK
    pend = jnp.cumsum(padded)
    pstart = pend - padded
    dest = pstart[top_idx] + rank
    n_blocks = -(-(n * TOP_K) // MOE_BLOCK) + N_EXPERTS
    cap = n_blocks * MOE_BLOCK
    flat_tok = jnp.repeat(jnp.arange(n, dtype=I32), TOP_K)
    tok_buf = jnp.zeros((cap,), I32).at[dest.reshape(-1)].set(flat_tok)
    gate_buf = jnp.zeros((cap,), F32).at[dest.reshape(-1)].set(gates.reshape(-1))
    blk_start = jnp.arange(n_blocks, dtype=I32) * MOE_BLOCK
    blk_e = jnp.minimum(jnp.sum((pend[None, :] <= blk_start[:, None]).astype(I32), axis=1), N_EXPERTS - 1)
    n_used = (pend[-1] // MOE_BLOCK).astype(I32).reshape(1)

    xs = _row_gather(hm.reshape(n, d // 128, 128), tok_buf).reshape(cap, d)
    y = _expert_ffn(xs, blk_e, n_used, w_gate_up.astype(BF16), b_gate_up, w_down.astype(BF16), b_down, gate_buf)
    return y, dest, x1


def kernel(x, norm_mix_w, w_in, q_norm_w, k_norm_w, out_norm_w, w_out, norm_moe_w, w_router, b_router,
           w_gate_up, b_gate_up, w_down, b_down, final_norm_w):
    b, s, d = x.shape
    n = b * s
    depth = w_in.shape[0]
    for l in range(depth):
        y, dest, x1 = _layer(x, norm_mix_w[l], w_in[l], q_norm_w[l], k_norm_w[l], out_norm_w[l], w_out[l],
                             norm_moe_w[l], w_router[l], b_router[l], w_gate_up[l], b_gate_up[l],
                             w_down[l], b_down[l])
        last = l == depth - 1
        w_fin = final_norm_w if last else jnp.ones((d,), F32)
        assert last, "only the final layer's combine is fused with a norm"
        x = _combine(y.reshape(-1, d // 128, 128), dest, x1.reshape(n, d // 128, 128), w_fin).reshape(b, s, d)
    return x
```

```python
import functools

import jax
import jax.numpy as jnp
from jax import lax
from jax.experimental import pallas as pl
from jax.experimental.pallas import tpu as pltpu

F32 = jnp.float32
BF16 = jnp.bfloat16
I32 = jnp.int32

HEAD_DIM = 64
GQA_RATIO = 4
B_CONFIGS = ((128, 1), (512, 4), (2048, 16))
GRID_W = 64
ROPE_THETA = 10000.0
N_EXPERTS = 32
TOP_K = 4
MOE_BLOCK = 512
SWIGLU_LIMIT = 7.0
SWIGLU_ALPHA = 1.702
NORM_EPS = 1e-5
QK_EPS = 1e-6
LOG2E = 1.4426950408889634
NEG_BIG = -1e30
V_PAD_ROWS = 16
MAX_SAFE_LOG2_SCORE = 40.0
V7X_VMEM_LIMIT = 56 * 1024 * 1024


def _params(*sem):
    return pltpu.CompilerParams(dimension_semantics=sem, vmem_limit_bytes=V7X_VMEM_LIMIT)


def _prenorm_body(x_ref, w_ref, o_ref):
    x = x_ref[...]
    ms = jnp.mean(x * x, axis=-1, keepdims=True)
    o_ref[...] = (x * lax.rsqrt(ms + NORM_EPS) * w_ref[...]).astype(o_ref.dtype)


def _prenorm(x2, w, tm=512):
    n, d = x2.shape
    return pl.pallas_call(
        _prenorm_body,
        grid=(n // tm,),
        in_specs=[pl.BlockSpec((tm, d), lambda i: (i, 0)), pl.BlockSpec((1, d), lambda i: (0, 0))],
        out_specs=pl.BlockSpec((tm, d), lambda i: (i, 0)),
        out_shape=jax.ShapeDtypeStruct((n, d), BF16),
        compiler_params=_params("parallel"),
        name="prenorm",
    )(x2, w.reshape(1, d))


def _aproj_body(n_q, n_kv, h_ref, wqv_ref, wk_ref, qnw_ref, knw_ref, cos_ref, sin_ref,
                kc_ref, ksm_ref, ksp_ref, bd_ref, qT_ref, k_ref, vT_ref, pt_ref):
    h = h_ref[0]
    pt_ref[...] = lax.dot_general(wqv_ref[...], h, (((1,), (1,)), ((), ())),
                                  preferred_element_type=F32)
    q4 = HEAD_DIM // 4
    cr, cc = cos_ref[0:q4, :], cos_ref[q4:2 * q4, :]
    sr, sc = sin_ref[0:q4, :], sin_ref[q4:2 * q4, :]
    qnw = qnw_ref[...]

    def q_head(hh, carry):
        y = pt_ref[pl.ds(pl.multiple_of(hh * HEAD_DIM, HEAD_DIM), HEAD_DIM), :]
        ms = jnp.mean(y * y, axis=0, keepdims=True)
        y = y * lax.rsqrt(ms + QK_EPS) * qnw
        a1, a2, b1, b2 = y[0:q4], y[q4:2 * q4], y[2 * q4:3 * q4], y[3 * q4:]
        out = jnp.concatenate([a1 * cr - a2 * sr, a2 * cr + a1 * sr,
                               b1 * cc - b2 * sc, b2 * cc + b1 * sc], axis=0)
        qT_ref[0, hh] = (out * (HEAD_DIM ** -0.5 * LOG2E)).astype(BF16)
        return carry

    lax.fori_loop(0, n_q, q_head, 0)
    ones_row = (lax.broadcasted_iota(I32, (V_PAD_ROWS, pt_ref.shape[1]), 0) == 0).astype(BF16)
    for g in range(n_kv):
        lo = (n_q + g) * HEAD_DIM
        vT_ref[0, g, :HEAD_DIM] = pt_ref[lo:lo + HEAD_DIM, :].astype(BF16)
        vT_ref[0, g, HEAD_DIM:] = ones_row

    kn = jnp.dot(h, wk_ref[...], preferred_element_type=F32)
    ms = jnp.dot(kn * kn, bd_ref[...], precision=lax.Precision.HIGHEST,
                 preferred_element_type=F32)
    kn = kn * lax.rsqrt(ms + QK_EPS) * knw_ref[...]
    for c in range(n_kv // 2):
        y = kn[:, c * 128:(c + 1) * 128]
        out = (y * kc_ref[...] + pltpu.roll(y, 128 - q4, 1) * ksm_ref[...]
               + pltpu.roll(y, q4, 1) * ksp_ref[...])
        k_ref[0, 2 * c] = out[:, :HEAD_DIM].astype(BF16)
        k_ref[0, 2 * c + 1] = out[:, HEAD_DIM:].astype(BF16)


def _rope_tables(seq_len):
    rows = seq_len // GRID_W
    r, c = jnp.meshgrid(jnp.arange(rows), jnp.arange(GRID_W), indexing="ij")
    axis_dim = HEAD_DIM // 2
    inv = ROPE_THETA ** (-jnp.arange(0, axis_dim, 2, dtype=F32) / axis_dim)
    ang_r = r.reshape(-1).astype(F32)[:, None] * inv[None, :]
    ang_c = c.reshape(-1).astype(F32)[:, None] * inv[None, :]
    ang = jnp.concatenate([ang_r, ang_c], axis=-1)
    return jnp.cos(ang), jnp.sin(ang)


def _aproj(h3, w_q, w_k, w_v, q_norm_w, k_norm_w, ts=512):
    b, s, d = h3.shape
    n_q, n_kv = w_q.shape[1] // HEAD_DIM, w_k.shape[1] // HEAD_DIM
    q4 = HEAD_DIM // 4
    cos, sin = _rope_tables(s)
    zeros = jnp.zeros_like(sin[:, :q4])
    c64 = jnp.concatenate([cos[:, :q4], cos[:, :q4], cos[:, q4:], cos[:, q4:]], axis=1)
    sm64 = jnp.concatenate([-sin[:, :q4], zeros, -sin[:, q4:], zeros], axis=1)
    sp64 = jnp.concatenate([zeros, sin[:, :q4], zeros, sin[:, q4:]], axis=1)
    kc, ksm, ksp = (jnp.concatenate([t, t], axis=1) for t in (c64, sm64, sp64))
    head_of = jnp.arange(n_kv * HEAD_DIM) // HEAD_DIM
    bd = (head_of[:, None] == head_of[None, :]).astype(F32) / HEAD_DIM
    wqv_t = jnp.concatenate([w_q, w_v], axis=1).T.astype(BF16)
    rows = wqv_t.shape[0]
    full = lambda shape: pl.BlockSpec(shape, lambda bi, i: (0,) * len(shape))
    return pl.pallas_call(
        functools.partial(_aproj_body, n_q, n_kv),
        grid=(b, s // ts),
        in_specs=[
            pl.BlockSpec((1, ts, d), lambda bi, i: (bi, i, 0)),
            full((rows, d)), full((d, n_kv * HEAD_DIM)),
            full((HEAD_DIM, 1)), full((1, n_kv * HEAD_DIM)),
            pl.BlockSpec((2 * q4, ts), lambda bi, i: (0, i)),
            pl.BlockSpec((2 * q4, ts), lambda bi, i: (0, i)),
            pl.BlockSpec((ts, 128), lambda bi, i: (i, 0)),
            pl.BlockSpec((ts, 128), lambda bi, i: (i, 0)),
            pl.BlockSpec((ts, 128), lambda bi, i: (i, 0)),
            full((n_kv * HEAD_DIM, n_kv * HEAD_DIM)),
        ],
        out_specs=[
            pl.BlockSpec((1, n_q, HEAD_DIM, ts), lambda bi, i: (bi, 0, 0, i)),
            pl.BlockSpec((1, n_kv, ts, HEAD_DIM), lambda bi, i: (bi, 0, i, 0)),
            pl.BlockSpec((1, n_kv, HEAD_DIM + V_PAD_ROWS, ts), lambda bi, i: (bi, 0, 0, i)),
        ],
        out_shape=[
            jax.ShapeDtypeStruct((b, n_q, HEAD_DIM, s), BF16),
            jax.ShapeDtypeStruct((b, n_kv, s, HEAD_DIM), BF16),
            jax.ShapeDtypeStruct((b, n_kv, HEAD_DIM + V_PAD_ROWS, s), BF16),
        ],
        scratch_shapes=[pltpu.VMEM((rows, ts), F32)],
        compiler_params=_params("parallel", "parallel"),
        name="aproj",
    )(h3, wqv_t, w_k.astype(BF16), q_norm_w.reshape(HEAD_DIM, 1),
      jnp.tile(k_norm_w, n_kv).reshape(1, n_kv * HEAD_DIM), cos.T, sin.T, kc, ksm, ksp, bd)


def _attn_a_body(tk, qT_ref, k_ref, vT_ref, o_ref, q_scr, acc_scr, sa_scr, sb_scr, p_scr):
    s_len = k_ref.shape[2]
    tq = qT_ref.shape[3]
    w = GQA_RATIO * tq
    n_steps = s_len // tk
    rc = 32
    for hh in range(GQA_RATIO):
        q_scr[:, hh * tq:(hh + 1) * tq] = qT_ref[0, hh]
    acc_scr[...] = jnp.zeros(acc_scr.shape, F32)

    def scores(j, s_scr):
        off = pl.multiple_of(j * tk, tk)
        s_scr[...] = jnp.dot(k_ref[0, 0, pl.ds(off, tk), :], q_scr[...], preferred_element_type=F32)

    def half_step(j, m, s_scr, nxt_scr):
        scores(jnp.minimum(j + 1, n_steps - 1), nxt_scr)
        off = pl.multiple_of(j * tk, tk)
        v = vT_ref[0, 0, :, pl.ds(off, tk)]
        m8 = jnp.full((8, w), NEG_BIG, F32)
        for c in range(tk // rc):
            sc = s_scr[c * rc:(c + 1) * rc, :]
            m8 = jnp.maximum(m8, jnp.max(sc.reshape(rc // 8, 8, w), axis=0))
        m_new = jnp.maximum(m, jnp.max(m8, axis=0, keepdims=True))
        for c in range(tk // rc):
            sc = s_scr[c * rc:(c + 1) * rc, :]
            p_scr[c * rc:(c + 1) * rc, :] = jnp.exp2(sc - m_new).astype(BF16)
        alpha = jnp.exp2(m - m_new)
        acc_scr[...] = alpha * acc_scr[...] + jnp.dot(v, p_scr[...], preferred_element_type=F32)
        return m_new

    def kv_step(jj, m):
        m = half_step(2 * jj, m, sa_scr, sb_scr)
        return half_step(2 * jj + 1, m, sb_scr, sa_scr)

    scores(0, sa_scr)
    lax.fori_loop(0, n_steps // 2, kv_step, jnp.full((1, w), NEG_BIG, F32))
    o = acc_scr[:HEAD_DIM, :] / acc_scr[HEAD_DIM:HEAD_DIM + 1, :]
    for hh in range(GQA_RATIO):
        o_ref[0, :, hh * HEAD_DIM:(hh + 1) * HEAD_DIM] = o[:, hh * tq:(hh + 1) * tq].T.astype(o_ref.dtype)


def _attn_a_bounded_body(tk, qT_ref, k_ref, vT_ref, o_ref, q_scr):
    s_len = k_ref.shape[2]
    tq = qT_ref.shape[3]
    w = GQA_RATIO * tq
    for hh in range(GQA_RATIO):
        q_scr[:, hh * tq:(hh + 1) * tq] = qT_ref[0, hh]

    def kv_step(j, acc):
        off = pl.multiple_of(j * tk, tk)
        s = jnp.dot(k_ref[0, 0, pl.ds(off, tk), :], q_scr[...], preferred_element_type=F32)
        p = jnp.exp2(s).astype(BF16)
        return acc + jnp.dot(vT_ref[0, 0, :, pl.ds(off, tk)], p, preferred_element_type=F32)

    acc = lax.fori_loop(0, s_len // tk, kv_step, jnp.zeros((vT_ref.shape[2], w), F32))
    o = acc[:HEAD_DIM] / acc[HEAD_DIM:HEAD_DIM + 1]
    for hh in range(GQA_RATIO):
        o_ref[0, :, hh * HEAD_DIM:(hh + 1) * HEAD_DIM] = o[:, hh * tq:(hh + 1) * tq].T.astype(o_ref.dtype)


def _attn_a(qT, k, vT, score_bound, tq=256, tk=256, tk_bounded=1024):
    b, n_q, _, s = qT.shape
    n_kv = k.shape[1]
    vr = vT.shape[2]
    w = GQA_RATIO * tq
    tk_bounded = min(tk_bounded, s)
    assert s % (2 * tk) == 0 and s % tq == 0 and s % tk_bounded == 0
    common = dict(
        grid=(b, n_kv, s // tq),
        in_specs=[
            pl.BlockSpec((1, GQA_RATIO, HEAD_DIM, tq), lambda bi, g, i: (bi, g, 0, i)),
            pl.BlockSpec((1, 1, s, HEAD_DIM), lambda bi, g, i: (bi, g, 0, 0)),
            pl.BlockSpec((1, 1, vr, s), lambda bi, g, i: (bi, g, 0, 0)),
        ],
        out_specs=pl.BlockSpec((1, tq, GQA_RATIO * HEAD_DIM), lambda bi, g, i: (bi, i, g)),
        out_shape=jax.ShapeDtypeStruct((b, s, n_q * HEAD_DIM), BF16),
        compiler_params=_params("parallel", "parallel", "parallel"),
    )
    general = pl.pallas_call(
        functools.partial(_attn_a_body, tk),
        scratch_shapes=[pltpu.VMEM((HEAD_DIM, w), BF16), pltpu.VMEM((vr, w), F32),
                        pltpu.VMEM((tk, w), F32), pltpu.VMEM((tk, w), F32), pltpu.VMEM((tk, w), BF16)],
        name="attn_a", **common)
    bounded = pl.pallas_call(
        functools.partial(_attn_a_bounded_body, tk_bounded),
        scratch_shapes=[pltpu.VMEM((HEAD_DIM, w), BF16)],
        name="attn_a_bounded", **common)
    return lax.cond(score_bound <= MAX_SAFE_LOG2_SCORE, bounded, general, qT, k, vT)


def _bproj_body(dilation, h_ref, w_ref, o_ref, scr):
    res = jnp.dot(h_ref[...], w_ref[...], preferred_element_type=F32)
    if dilation == 1:
        o_ref[0] = res.astype(o_ref.dtype)
    else:
        rows = scr.shape[1] // dilation
        for c in range(scr.shape[0]):
            scr[c] = res[:, c * 128:(c + 1) * 128]
        for r in range(dilation):
            for c in range(scr.shape[0]):
                o_ref[r, :, c * 128:(c + 1) * 128] = scr[c, pl.ds(r, rows, stride=dilation), :].astype(o_ref.dtype)


def _bproj(h, w, batch, dilation, tm=512):
    n, d = h.shape
    e = w.shape[1]
    s = n // batch
    per_b = s // tm
    assert tm % (16 * dilation) == 0
    return pl.pallas_call(
        functools.partial(_bproj_body, dilation),
        grid=(n // tm,),
        in_specs=[pl.BlockSpec((tm, d), lambda i: (i, 0)), pl.BlockSpec((d, e), lambda i: (0, 0))],
        out_specs=pl.BlockSpec((None, dilation, tm // dilation, e),
                               lambda i: (i // per_b, 0, i % per_b, 0)),
        out_shape=jax.ShapeDtypeStruct((batch, dilation, s // dilation, e), BF16),
        scratch_shapes=[pltpu.VMEM((e // 128, tm, 128), F32)],
        compiler_params=_params("parallel"),
        name=f"bproj_d{dilation}",
    )(h, w)


def _attn_b_body(dilation, radius, q_ref, kp_ref, kc_ref, kn_ref, vp_ref, vc_ref, vn_ref,
                 o_ref, lse_ref):
    tu = q_ref.shape[0]
    n_pairs = q_ref.shape[1] // 128
    i = pl.program_id(2)
    n_u = pl.num_programs(2) * tu
    row = lax.broadcasted_iota(I32, (tu, 2 * tu), 0)
    col = lax.broadcasted_iota(I32, (tu, 2 * tu), 1)
    j = col - radius - row
    key = i * tu - radius + col
    valid = (jnp.abs(j) <= radius) & (key >= 0) & (key < n_u)
    dist = (dilation * jnp.abs(j)).astype(F32)
    pen = jnp.where(valid, 0.0, NEG_BIG)
    first = lax.broadcasted_iota(I32, (1, 128), 1) < HEAD_DIM
    scale = HEAD_DIM ** -0.5
    n_heads = 2 * n_pairs
    for hp in range(n_pairs):
        lanes = slice(hp * 128, (hp + 1) * 128)
        q2 = q_ref[:, lanes]
        kw = jnp.concatenate([kp_ref[tu - radius:, lanes], kc_ref[:, lanes],
                              kn_ref[:tu - radius, lanes]], axis=0)
        vw = jnp.concatenate([vp_ref[tu - radius:, lanes], vc_ref[:, lanes],
                              vn_ref[:tu - radius, lanes]], axis=0)
        o2 = jnp.zeros((tu, 128), F32)
        lse2 = jnp.zeros((tu, 128), F32)
        for sub in range(2):
            sel = first if sub == 0 else jnp.logical_not(first)
            slope = 2.0 ** (-8.0 * (2 * hp + sub + 1) / n_heads)
            qm = jnp.where(sel, q2, jnp.zeros_like(q2))
            s = lax.dot_general(qm, kw, (((1,), (1,)), ((), ())), preferred_element_type=F32)
            s = s * scale + (pen - slope * dist)
            m = jnp.max(s, axis=1, keepdims=True)
            p = jnp.exp(s - m)
            l = jnp.sum(p, axis=1, keepdims=True)
            vm = jnp.where(sel, vw, jnp.zeros_like(vw))
            o2 = o2 + jnp.dot(p.astype(BF16), vm, preferred_element_type=F32) / l
            lse2 = jnp.where(sel, m + jnp.log(l), lse2)
        o_ref[:, lanes] = o2.astype(o_ref.dtype)
        lse_ref[:, lanes] = lse2


def _attn_b(pbd, window, tu=128):
    b, dilation, u, e = pbd.shape
    width = e // 3
    radius = window // (2 * dilation)
    assert radius < tu and u % tu == 0
    n_t = u // tu

    def spec(which, shift):
        def index(bi, r, i):
            return (bi, r, jnp.clip(i + shift, 0, n_t - 1), which)
        return pl.BlockSpec((None, None, tu, width), index)

    out_spec = pl.BlockSpec((None, None, tu, width), lambda bi, r, i: (bi, r, i, 0))
    return pl.pallas_call(
        functools.partial(_attn_b_body, dilation, radius),
        grid=(b, dilation, n_t),
        in_specs=[spec(0, 0), spec(1, -1), spec(1, 0), spec(1, 1), spec(2, -1), spec(2, 0), spec(2, 1)],
        out_specs=[out_spec, out_spec],
        out_shape=[jax.ShapeDtypeStruct((b, dilation, u, width), BF16),
                   jax.ShapeDtypeStruct((b, dilation, u, width), F32)],
        compiler_params=_params("parallel", "parallel", "parallel"),
        name=f"attn_b_d{dilation}",
    )(pbd, pbd, pbd, pbd, pbd, pbd, pbd)


def _rms(x, w):
    ms = jnp.mean(x * x, axis=-1, keepdims=True)
    return x * lax.rsqrt(ms + NORM_EPS) * w


def _token_order(ref, scr):
    d, rows, _ = ref.shape
    if d == 1:
        return ref[0].astype(F32)
    for r in range(d):
        slab = ref[r].astype(F32)
        for c in range(scr.shape[0]):
            scr[c, pl.ds(r, rows, stride=d), :] = slab[:, c * 128:(c + 1) * 128]
    return jnp.concatenate([scr[c] for c in range(scr.shape[0])], axis=1)


def _outproj_body(n_groups, *refs):
    oa_ref = refs[0]
    ob_refs = refs[1:1 + n_groups]
    lse_refs = refs[1 + n_groups:1 + 2 * n_groups]
    (x_ref, wo_a_ref, wo_b_ref, nwa_ref, nwb_ref, nm_ref, wr_ref, br_ref,
     x1_ref, hm_ref, lg_ref) = refs[1 + 2 * n_groups:12 + 2 * n_groups]
    scrs = refs[12 + 2 * n_groups:]
    obs = [_token_order(r, scrs[2 * g]) for g, r in enumerate(ob_refs)]
    lses = [_token_order(r, scrs[2 * g + 1]) for g, r in enumerate(lse_refs)]
    mx = functools.reduce(jnp.maximum, lses)
    ws = [jnp.exp(v - mx) for v in lses]
    den = functools.reduce(lambda a, c: a + c, ws)
    ob = functools.reduce(lambda a, c: a + c, [w * o for w, o in zip(ws, obs)]) / den
    na = _rms(oa_ref[...].astype(F32), nwa_ref[...]).astype(BF16)
    nb = _rms(ob, nwb_ref[...]).astype(BF16)
    x1 = (x_ref[...] + jnp.dot(na, wo_a_ref[...], preferred_element_type=F32)
          + jnp.dot(nb, wo_b_ref[...], preferred_element_type=F32))
    x1_ref[...] = x1
    hm = _rms(x1, nm_ref[...])
    hm_ref[...] = hm.astype(BF16)
    lg_ref[...] = jnp.dot(hm, wr_ref[...], precision=lax.Precision.HIGHEST,
                          preferred_element_type=F32) + br_ref[...]


def _outproj(o_a, o_bs, lses, x2, w_out, out_norm_w, norm_moe_w, w_router, b_router, tm=256):
    n, d = x2.shape
    wa, wb = o_a.shape[1], o_bs[0].shape[3]
    n_e = w_router.shape[1]
    per_b = n // o_bs[0].shape[0] // tm
    row = lambda width: pl.BlockSpec((tm, width), lambda i: (i, 0))
    full = lambda shape: pl.BlockSpec(shape, lambda i: (0,) * len(shape))
    grouped = [pl.BlockSpec((None, o.shape[1], tm // o.shape[1], wb),
                            lambda i: (i // per_b, 0, i % per_b, 0)) for o in o_bs]
    n_groups = len(o_bs)
    return pl.pallas_call(
        functools.partial(_outproj_body, n_groups),
        grid=(n // tm,),
        in_specs=[row(wa)] + grouped + grouped + [
            row(d), full((wa, d)), full((wb, d)), full((1, wa)), full((1, wb)), full((1, d)),
            full((d, n_e)), full((1, n_e))],
        out_specs=[row(d), row(d), row(n_e)],
        out_shape=[jax.ShapeDtypeStruct((n, d), F32), jax.ShapeDtypeStruct((n, d), BF16),
                   jax.ShapeDtypeStruct((n, n_e), F32)],
        scratch_shapes=[pltpu.VMEM((wb // 128, tm, 128), F32)] * (2 * n_groups),
        compiler_params=_params("parallel"),
        name="outproj",
    )(o_a, *o_bs, *lses, x2, w_out[:wa].astype(BF16), w_out[wa:].astype(BF16),
      out_norm_w[:wa].reshape(1, wa), out_norm_w[wa:].reshape(1, wb), norm_moe_w.reshape(1, d),
      w_router, b_router.reshape(1, n_e))


def _router_body(lg_ref, idx_ref, gate_ref, rank_ref, cnt_ref, carry_ref):
    i = pl.program_id(0)
    tm, n_e = lg_ref.shape

    @pl.when(i == 0)
    def _():
        carry_ref[...] = jnp.zeros_like(carry_ref)

    lane = lax.broadcasted_iota(I32, (tm, n_e), 1)
    work = lg_ref[...]
    vals, idxs = [], []
    for _ in range(TOP_K):
        mx = jnp.max(work, axis=1, keepdims=True)
        ix = jnp.min(jnp.where(work == mx, lane, n_e), axis=1, keepdims=True)
        vals.append(mx)
        idxs.append(ix)
        work = jnp.where(lane == ix, -jnp.inf, work)
    es = [jnp.exp(v - vals[0]) for v in vals]
    den = functools.reduce(lambda a, c: a + c, es)
    chosen = functools.reduce(lambda a, c: a + c, [(lane == ix).astype(F32) for ix in idxs])
    r = lax.broadcasted_iota(I32, (tm, tm), 0)
    c = lax.broadcasted_iota(I32, (tm, tm), 1)
    before = jnp.where(c < r, 1.0, 0.0).astype(BF16)
    prefix = jnp.dot(before, chosen.astype(BF16), preferred_element_type=F32) + carry_ref[...]
    for k in range(TOP_K):
        idx_ref[:, k:k + 1] = idxs[k]
        gate_ref[:, k:k + 1] = es[k] / den
        rank_ref[:, k:k + 1] = jnp.sum(jnp.where(lane == idxs[k], prefix, 0.0), axis=1,
                                       keepdims=True).astype(I32)
    carry_ref[...] += jnp.sum(chosen, axis=0, keepdims=True)
    cnt_ref[...] = carry_ref[...].astype(I32)


def _router(logits, tm=256):
    n, n_e = logits.shape
    row = pl.BlockSpec((tm, TOP_K), lambda i: (i, 0))
    return pl.pallas_call(
        _router_body,
        grid=(n // tm,),
        in_specs=[pl.BlockSpec((tm, n_e), lambda i: (i, 0))],
        out_specs=[row, row, row, pl.BlockSpec((1, n_e), lambda i: (0, 0))],
        out_shape=[jax.ShapeDtypeStruct((n, TOP_K), I32), jax.ShapeDtypeStruct((n, TOP_K), F32),
                   jax.ShapeDtypeStruct((n, TOP_K), I32), jax.ShapeDtypeStruct((1, n_e), I32)],
        scratch_shapes=[pltpu.VMEM((1, n_e), F32)],
        compiler_params=_params("arbitrary"),
        name="router",
    )(logits)


def _row_gather_body(rows_per_step, idx_ref, src_ref, dst_ref, sem):
    def issue(r, carry):
        pltpu.make_async_copy(src_ref.at[idx_ref[0, 0, r]], dst_ref.at[r], sem).start()
        return carry

    lax.fori_loop(0, rows_per_step, issue, 0, unroll=8)
    pltpu.make_async_copy(src_ref.at[pl.ds(0, rows_per_step)], dst_ref, sem).wait()


def _row_gather(src3, idx, rows_per_step=512):
    n_out = idx.shape[0]
    steps = n_out // rows_per_step
    slab = src3.shape[1:]
    return pl.pallas_call(
        functools.partial(_row_gather_body, rows_per_step),
        grid=(steps,),
        in_specs=[pl.BlockSpec((1, 1, rows_per_step), lambda i: (i, 0, 0), memory_space=pltpu.SMEM),
                  pl.BlockSpec(memory_space=pl.ANY)],
        out_specs=pl.BlockSpec((rows_per_step,) + slab, lambda i: (i, 0, 0)),
        out_shape=jax.ShapeDtypeStruct((n_out,) + slab, src3.dtype),
        scratch_shapes=[pltpu.SemaphoreType.DMA(())],
        compiler_params=_params("arbitrary"),
        name="dispatch",
    )(idx.reshape(steps, 1, rows_per_step), src3)


def _ffn_body(blk_e_ref, n_used_ref, x_ref, wg_ref, wu_ref, bg_ref, bu_ref, wd_ref, bd_ref, g_ref,
              y_ref, acc_ref):
    i, f = pl.program_id(0), pl.program_id(1)
    n_f = pl.num_programs(1)
    used = i < n_used_ref[0]

    @pl.when(used)
    def _():
        x = x_ref[...]
        gate = jnp.dot(x, wg_ref[0], preferred_element_type=F32) + bg_ref[0]
        up = jnp.dot(x, wu_ref[0], preferred_element_type=F32) + bu_ref[0]
        gate = jnp.minimum(gate, SWIGLU_LIMIT)
        up = jnp.clip(up, -SWIGLU_LIMIT, SWIGLU_LIMIT)
        act = (up + 1.0) * gate * jax.nn.sigmoid(SWIGLU_ALPHA * gate)
        part = jnp.dot(act.astype(BF16), wd_ref[0], preferred_element_type=F32)

        @pl.when(f == 0)
        def _():
            acc_ref[...] = part

        @pl.when(f > 0)
        def _():
            acc_ref[...] += part

        @pl.when(f == n_f - 1)
        def _():
            y_ref[...] = ((acc_ref[...] + bd_ref[0]) * g_ref[...]).astype(y_ref.dtype)

    @pl.when(jnp.logical_not(used) & (f == n_f - 1))
    def _():
        y_ref[...] = jnp.zeros_like(y_ref)


def _expert_ffn(xs, blk_e, n_used, w_gu, b_gu, w_down, b_down, gate_buf, tf=512):
    cap, d = xs.shape
    n_e, _, two_f = w_gu.shape
    ff = two_f // 2
    n_f = ff // tf
    n_blocks = cap // MOE_BLOCK

    def fidx(i, f, nu):
        return jnp.where(i < nu[0], f, n_f - 1)

    grid_spec = pltpu.PrefetchScalarGridSpec(
        num_scalar_prefetch=2,
        grid=(n_blocks, n_f),
        in_specs=[
            pl.BlockSpec((MOE_BLOCK, d), lambda i, f, be, nu: (jnp.minimum(i, nu[0] - 1), 0)),
            pl.BlockSpec((1, d, tf), lambda i, f, be, nu: (be[i], 0, fidx(i, f, nu))),
            pl.BlockSpec((1, d, tf), lambda i, f, be, nu: (be[i], 0, n_f + fidx(i, f, nu))),
            pl.BlockSpec((1, 1, tf), lambda i, f, be, nu: (be[i], 0, fidx(i, f, nu))),
            pl.BlockSpec((1, 1, tf), lambda i, f, be, nu: (be[i], 0, n_f + fidx(i, f, nu))),
            pl.BlockSpec((1, tf, d), lambda i, f, be, nu: (be[i], fidx(i, f, nu), 0)),
            pl.BlockSpec((1, 1, d), lambda i, f, be, nu: (be[i], 0, 0)),
            pl.BlockSpec((MOE_BLOCK, 1), lambda i, f, be, nu: (i, 0)),
        ],
        out_specs=pl.BlockSpec((MOE_BLOCK, d), lambda i, f, be, nu: (i, 0)),
        scratch_shapes=[pltpu.VMEM((MOE_BLOCK, d), F32)],
    )
    return pl.pallas_call(
        _ffn_body,
        grid_spec=grid_spec,
        out_shape=jax.ShapeDtypeStruct((cap, d), F32),
        compiler_params=_params("arbitrary", "arbitrary"),
        name="expert_ffn",
    )(blk_e, n_used, xs, w_gu, w_gu, b_gu.reshape(n_e, 1, two_f), b_gu.reshape(n_e, 1, two_f),
      w_down, b_down.reshape(n_e, 1, d), gate_buf.reshape(cap, 1))


def _combine_body(tm, dest_ref, y_ref, x1_ref, w_ref, o_ref, buf_ref, sem):
    def issue(t, carry):
        for k in range(TOP_K):
            pltpu.make_async_copy(y_ref.at[dest_ref[0, 0, t * TOP_K + k]], buf_ref.at[k, t], sem).start()
        return carry

    lax.fori_loop(0, tm, issue, 0, unroll=2)
    for k in range(TOP_K):
        pltpu.make_async_copy(y_ref.at[pl.ds(0, tm)], buf_ref.at[k], sem).wait()
    x = x1_ref[...]
    for k in range(TOP_K):
        x = x + buf_ref[k]
    ms = jnp.sum(jnp.sum(x * x, axis=2, keepdims=True), axis=1, keepdims=True) / (x.shape[1] * x.shape[2])
    o_ref[...] = x * lax.rsqrt(ms + NORM_EPS) * w_ref[...]


def _combine(y3, dest, x13, w, tm=256):
    n, sl, _ = x13.shape
    steps = n // tm
    return pl.pallas_call(
        functools.partial(_combine_body, tm),
        grid=(steps,),
        in_specs=[pl.BlockSpec((1, 1, tm * TOP_K), lambda i: (i, 0, 0), memory_space=pltpu.SMEM),
                  pl.BlockSpec(memory_space=pl.ANY),
                  pl.BlockSpec((tm, sl, 128), lambda i: (i, 0, 0)),
                  pl.BlockSpec((1, sl, 128), lambda i: (0, 0, 0))],
        out_specs=pl.BlockSpec((tm, sl, 128), lambda i: (i, 0, 0)),
        out_shape=jax.ShapeDtypeStruct((n, sl, 128), F32),
        scratch_shapes=[pltpu.VMEM((TOP_K, tm, sl, 128), F32), pltpu.SemaphoreType.DMA(())],
        compiler_params=_params("arbitrary"),
        name="combine",
    )(dest.reshape(steps, 1, tm * TOP_K), y3, x13, w.reshape(1, sl, 128))


def _layer(x, norm_mix_w, w_in, q_norm_w, k_norm_w, out_norm_w, w_out, norm_moe_w,
           w_router, b_router, w_gate_up, b_gate_up, w_down, b_down):
    b, s, d = x.shape
    n = b * s
    x2 = x.reshape(n, d)
    b_w = d // 4
    a_q_w = d - b_w
    a_kv_w = a_q_w // GQA_RATIO

    h = _prenorm(x2, norm_mix_w)
    qT, k, vT = _aproj(h.reshape(b, s, d), w_in[:, :a_q_w], w_in[:, a_q_w:a_q_w + a_kv_w],
                       w_in[:, a_q_w + a_kv_w:a_q_w + 2 * a_kv_w], q_norm_w, k_norm_w)
    score_bound = (1.02 * HEAD_DIM ** 0.5 * LOG2E) * jnp.max(jnp.abs(q_norm_w)) * jnp.max(jnp.abs(k_norm_w))
    o_a = _attn_a(qT, k, vT, score_bound).reshape(n, a_q_w)

    o_bs, lses = [], []
    base = a_q_w + 2 * a_kv_w
    for g, (window, dilation) in enumerate(B_CONFIGS):
        w_g = w_in[:, base + 3 * g * b_w:base + 3 * (g + 1) * b_w].astype(BF16)
        o_g, lse_g = _attn_b(_bproj(h, w_g, b, dilation), window)
        o_bs.append(o_g)
        lses.append(lse_g)

    x1, hm, logits = _outproj(o_a, o_bs, lses, x2, w_out, out_norm_w, norm_moe_w, w_router, b_router)

    top_idx, gates, rank, counts = _router(logits)
    counts = counts.reshape(N_EXPERTS)
    padded = (counts + MOE_BLOCK - 1) // MOE_BLOCK * MOE_BLOCK
    pend = jnp.cumsum(padded)
    pstart = pend - padded
    dest = pstart[top_idx] + rank
    n_blocks = -(-(n * TOP_K) // MOE_BLOCK) + N_EXPERTS
    cap = n_blocks * MOE_BLOCK
    flat_tok = jnp.repeat(jnp.arange(n, dtype=I32), TOP_K)
    tok_buf = jnp.zeros((cap,), I32).at[dest.reshape(-1)].set(flat_tok)
    gate_buf = jnp.zeros((cap,), F32).at[dest.reshape(-1)].set(gates.reshape(-1))
    blk_start = jnp.arange(n_blocks, dtype=I32) * MOE_BLOCK
    blk_e = jnp.minimum(jnp.sum((pend[None, :] <= blk_start[:, None]).astype(I32), axis=1), N_EXPERTS - 1)
    n_used = (pend[-1] // MOE_BLOCK).astype(I32).reshape(1)

    xs = _row_gather(hm.reshape(n, d // 128, 128), tok_buf).reshape(cap, d)
    y = _expert_ffn(xs, blk_e, n_used, w_gate_up.astype(BF16), b_gate_up, w_down.astype(BF16), b_down, gate_buf)
    return y, dest, x1


def kernel(x, norm_mix_w, w_in, q_norm_w, k_norm_w, out_norm_w, w_out, norm_moe_w, w_router, b_router,
           w_gate_up, b_gate_up, w_down, b_down, final_norm_w):
    b, s, d = x.shape
    n = b * s
    assert w_in.shape[0] == 1, "the MoE combine is fused with the final norm: single-layer stacks only"
    y, dest, x1 = _layer(x, norm_mix_w[0], w_in[0], q_norm_w[0], k_norm_w[0], out_norm_w[0], w_out[0],
                         norm_moe_w[0], w_router[0], b_router[0], w_gate_up[0], b_gate_up[0],
                         w_down[0], b_down[0])
    out = _combine(y.reshape(-1, d // 128, 128), dest, x1.reshape(n, d // 128, 128), final_norm_w)
    return out.reshape(b, s, d)
```

```python
import functools

import jax
import jax.numpy as jnp
from jax import lax
from jax.experimental import pallas as pl
from jax.experimental.pallas import tpu as pltpu

F32 = jnp.float32
BF16 = jnp.bfloat16
I32 = jnp.int32

HEAD_DIM = 64
GQA_RATIO = 4
B_CONFIGS = ((128, 1), (512, 4), (2048, 16))
GRID_W = 64
ROPE_THETA = 10000.0
N_EXPERTS = 32
TOP_K = 4
MOE_BLOCK = 512
SWIGLU_LIMIT = 7.0
SWIGLU_ALPHA = 1.702
NORM_EPS = 1e-5
QK_EPS = 1e-6
LOG2E = 1.4426950408889634
NEG_BIG = -1e30
V_PAD_ROWS = 16
MAX_SAFE_LOG2_SCORE = 40.0
V7X_VMEM_LIMIT = 56 * 1024 * 1024


def _params(*sem):
    return pltpu.CompilerParams(dimension_semantics=sem, vmem_limit_bytes=V7X_VMEM_LIMIT)


def _prenorm_body(x_ref, w_ref, o_ref):
    x = x_ref[...]
    ms = jnp.mean(x * x, axis=-1, keepdims=True)
    o_ref[...] = (x * lax.rsqrt(ms + NORM_EPS) * w_ref[...]).astype(o_ref.dtype)


def _prenorm(x2, w, tm=512):
    n, d = x2.shape
    return pl.pallas_call(
        _prenorm_body,
        grid=(n // tm,),
        in_specs=[pl.BlockSpec((tm, d), lambda i: (i, 0)), pl.BlockSpec((1, d), lambda i: (0, 0))],
        out_specs=pl.BlockSpec((tm, d), lambda i: (i, 0)),
        out_shape=jax.ShapeDtypeStruct((n, d), BF16),
        compiler_params=_params("parallel"),
        name="prenorm",
    )(x2, w.reshape(1, d))


def _aproj_body(n_q, n_kv, h_ref, wqv_ref, wk_ref, qnw_ref, knw_ref, cos_ref, sin_ref,
                kc_ref, ksm_ref, ksp_ref, bd_ref, qT_ref, k_ref, vT_ref, pt_ref):
    h = h_ref[0]
    pt_ref[...] = lax.dot_general(wqv_ref[...], h, (((1,), (1,)), ((), ())),
                                  preferred_element_type=F32)
    q4 = HEAD_DIM // 4
    cr, cc = cos_ref[0:q4, :], cos_ref[q4:2 * q4, :]
    sr, sc = sin_ref[0:q4, :], sin_ref[q4:2 * q4, :]
    qnw = qnw_ref[...]

    def q_head(hh, carry):
        y = pt_ref[pl.ds(pl.multiple_of(hh * HEAD_DIM, HEAD_DIM), HEAD_DIM), :]
        ms = jnp.mean(y * y, axis=0, keepdims=True)
        y = y * lax.rsqrt(ms + QK_EPS) * qnw
        a1, a2, b1, b2 = y[0:q4], y[q4:2 * q4], y[2 * q4:3 * q4], y[3 * q4:]
        out = jnp.concatenate([a1 * cr - a2 * sr, a2 * cr + a1 * sr,
                               b1 * cc - b2 * sc, b2 * cc + b1 * sc], axis=0)
        qT_ref[0, hh] = (out * (HEAD_DIM ** -0.5 * LOG2E)).astype(BF16)
        return carry

    lax.fori_loop(0, n_q, q_head, 0)
    ones_row = (lax.broadcasted_iota(I32, (V_PAD_ROWS, pt_ref.shape[1]), 0) == 0).astype(BF16)
    for g in range(n_kv):
        lo = (n_q + g) * HEAD_DIM
        vT_ref[0, g, :HEAD_DIM] = pt_ref[lo:lo + HEAD_DIM, :].astype(BF16)
        vT_ref[0, g, HEAD_DIM:] = ones_row

    kn = jnp.dot(h, wk_ref[...], preferred_element_type=F32)
    ms = jnp.dot(kn * kn, bd_ref[...], precision=lax.Precision.HIGHEST,
                 preferred_element_type=F32)
    kn = kn * lax.rsqrt(ms + QK_EPS) * knw_ref[...]
    for c in range(n_kv // 2):
        y = kn[:, c * 128:(c + 1) * 128]
        out = (y * kc_ref[...] + pltpu.roll(y, 128 - q4, 1) * ksm_ref[...]
               + pltpu.roll(y, q4, 1) * ksp_ref[...])
        k_ref[0, 2 * c] = out[:, :HEAD_DIM].astype(BF16)
        k_ref[0, 2 * c + 1] = out[:, HEAD_DIM:].astype(BF16)


def _rope_tables(seq_len):
    rows = seq_len // GRID_W
    r, c = jnp.meshgrid(jnp.arange(rows), jnp.arange(GRID_W), indexing="ij")
    axis_dim = HEAD_DIM // 2
    inv = ROPE_THETA ** (-jnp.arange(0, axis_dim, 2, dtype=F32) / axis_dim)
    ang_r = r.reshape(-1).astype(F32)[:, None] * inv[None, :]
    ang_c = c.reshape(-1).astype(F32)[:, None] * inv[None, :]
    ang = jnp.concatenate([ang_r, ang_c], axis=-1)
    return jnp.cos(ang), jnp.sin(ang)


def _aproj(h3, w_q, w_k, w_v, q_norm_w, k_norm_w, ts=512):
    b, s, d = h3.shape
    n_q, n_kv = w_q.shape[1] // HEAD_DIM, w_k.shape[1] // HEAD_DIM
    q4 = HEAD_DIM // 4
    cos, sin = _rope_tables(s)
    zeros = jnp.zeros_like(sin[:, :q4])
    c64 = jnp.concatenate([cos[:, :q4], cos[:, :q4], cos[:, q4:], cos[:, q4:]], axis=1)
    sm64 = jnp.concatenate([-sin[:, :q4], zeros, -sin[:, q4:], zeros], axis=1)
    sp64 = jnp.concatenate([zeros, sin[:, :q4], zeros, sin[:, q4:]], axis=1)
    kc, ksm, ksp = (jnp.concatenate([t, t], axis=1) for t in (c64, sm64, sp64))
    head_of = jnp.arange(n_kv * HEAD_DIM) // HEAD_DIM
    bd = (head_of[:, None] == head_of[None, :]).astype(F32) / HEAD_DIM
    wqv_t = jnp.concatenate([w_q, w_v], axis=1).T.astype(BF16)
    rows = wqv_t.shape[0]
    full = lambda shape: pl.BlockSpec(shape, lambda bi, i: (0,) * len(shape))
    return pl.pallas_call(
        functools.partial(_aproj_body, n_q, n_kv),
        grid=(b, s // ts),
        in_specs=[
            pl.BlockSpec((1, ts, d), lambda bi, i: (bi, i, 0)),
            full((rows, d)), full((d, n_kv * HEAD_DIM)),
            full((HEAD_DIM, 1)), full((1, n_kv * HEAD_DIM)),
            pl.BlockSpec((2 * q4, ts), lambda bi, i: (0, i)),
            pl.BlockSpec((2 * q4, ts), lambda bi, i: (0, i)),
            pl.BlockSpec((ts, 128), lambda bi, i: (i, 0)),
            pl.BlockSpec((ts, 128), lambda bi, i: (i, 0)),
            pl.BlockSpec((ts, 128), lambda bi, i: (i, 0)),
            full((n_kv * HEAD_DIM, n_kv * HEAD_DIM)),
        ],
        out_specs=[
            pl.BlockSpec((1, n_q, HEAD_DIM, ts), lambda bi, i: (bi, 0, 0, i)),
            pl.BlockSpec((1, n_kv, ts, HEAD_DIM), lambda bi, i: (bi, 0, i, 0)),
            pl.BlockSpec((1, n_kv, HEAD_DIM + V_PAD_ROWS, ts), lambda bi, i: (bi, 0, 0, i)),
        ],
        out_shape=[
            jax.ShapeDtypeStruct((b, n_q, HEAD_DIM, s), BF16),
            jax.ShapeDtypeStruct((b, n_kv, s, HEAD_DIM), BF16),
            jax.ShapeDtypeStruct((b, n_kv, HEAD_DIM + V_PAD_ROWS, s), BF16),
        ],
        scratch_shapes=[pltpu.VMEM((rows, ts), F32)],
        compiler_params=_params("parallel", "parallel"),
        name="aproj",
    )(h3, wqv_t, w_k.astype(BF16), q_norm_w.reshape(HEAD_DIM, 1),
      jnp.tile(k_norm_w, n_kv).reshape(1, n_kv * HEAD_DIM), cos.T, sin.T, kc, ksm, ksp, bd)


def _attn_a_body(tk, qT_ref, k_ref, vT_ref, o_ref, q_scr, acc_scr, sa_scr, sb_scr, p_scr):
    s_len = k_ref.shape[2]
    tq = qT_ref.shape[3]
    w = GQA_RATIO * tq
    n_steps = s_len // tk
    rc = 32
    for hh in range(GQA_RATIO):
        q_scr[:, hh * tq:(hh + 1) * tq] = qT_ref[0, hh]
    acc_scr[...] = jnp.zeros(acc_scr.shape, F32)

    def scores(j, s_scr):
        off = pl.multiple_of(j * tk, tk)
        s_scr[...] = jnp.dot(k_ref[0, 0, pl.ds(off, tk), :], q_scr[...], preferred_element_type=F32)

    def half_step(j, m, s_scr, nxt_scr):
        scores(jnp.minimum(j + 1, n_steps - 1), nxt_scr)
        off = pl.multiple_of(j * tk, tk)
        v = vT_ref[0, 0, :, pl.ds(off, tk)]
        m8 = jnp.full((8, w), NEG_BIG, F32)
        for c in range(tk // rc):
            sc = s_scr[c * rc:(c + 1) * rc, :]
            m8 = jnp.maximum(m8, jnp.max(sc.reshape(rc // 8, 8, w), axis=0))
        m_new = jnp.maximum(m, jnp.max(m8, axis=0, keepdims=True))
        for c in range(tk // rc):
            sc = s_scr[c * rc:(c + 1) * rc, :]
            p_scr[c * rc:(c + 1) * rc, :] = jnp.exp2(sc - m_new).astype(BF16)
        alpha = jnp.exp2(m - m_new)
        acc_scr[...] = alpha * acc_scr[...] + jnp.dot(v, p_scr[...], preferred_element_type=F32)
        return m_new

    def kv_step(jj, m):
        m = half_step(2 * jj, m, sa_scr, sb_scr)
        return half_step(2 * jj + 1, m, sb_scr, sa_scr)

    scores(0, sa_scr)
    lax.fori_loop(0, n_steps // 2, kv_step, jnp.full((1, w), NEG_BIG, F32))
    o = acc_scr[:HEAD_DIM, :] / acc_scr[HEAD_DIM:HEAD_DIM + 1, :]
    for hh in range(GQA_RATIO):
        o_ref[0, :, hh * HEAD_DIM:(hh + 1) * HEAD_DIM] = o[:, hh * tq:(hh + 1) * tq].T.astype(o_ref.dtype)


def _attn_a_bounded_body(tk, qT_ref, k_ref, vT_ref, o_ref, q_scr, pa_scr, pb_scr, acc_scr):
    s_len = k_ref.shape[2]
    tq = qT_ref.shape[3]
    n_steps = s_len // tk
    for hh in range(GQA_RATIO):
        q_scr[:, hh * tq:(hh + 1) * tq] = qT_ref[0, hh]
    acc_scr[...] = jnp.zeros(acc_scr.shape, F32)

    def probs(j, p_scr):
        off = pl.multiple_of(j * tk, tk)
        s = jnp.dot(k_ref[0, 0, pl.ds(off, tk), :], q_scr[...], preferred_element_type=F32)
        p_scr[...] = jnp.exp2(s).astype(BF16)

    def half_step(j, p_scr, nxt_scr):
        probs(jnp.minimum(j + 1, n_steps - 1), nxt_scr)
        off = pl.multiple_of(j * tk, tk)
        acc_scr[...] += jnp.dot(vT_ref[0, 0, :, pl.ds(off, tk)], p_scr[...], preferred_element_type=F32)

    def kv_step(jj, carry):
        half_step(2 * jj, pa_scr, pb_scr)
        half_step(2 * jj + 1, pb_scr, pa_scr)
        return carry

    probs(0, pa_scr)
    lax.fori_loop(0, n_steps // 2, kv_step, 0)
    o = acc_scr[:HEAD_DIM, :] / acc_scr[HEAD_DIM:HEAD_DIM + 1, :]
    for hh in range(GQA_RATIO):
        o_ref[0, :, hh * HEAD_DIM:(hh + 1) * HEAD_DIM] = o[:, hh * tq:(hh + 1) * tq].T.astype(o_ref.dtype)


def _attn_a_call(body, tq, scratch, name, qT, k, vT):
    b, n_q, _, s = qT.shape
    n_kv, vr = k.shape[1], vT.shape[2]
    return pl.pallas_call(
        body,
        grid=(b, n_kv, s // tq),
        in_specs=[
            pl.BlockSpec((1, GQA_RATIO, HEAD_DIM, tq), lambda bi, g, i: (bi, g, 0, i)),
            pl.BlockSpec((1, 1, s, HEAD_DIM), lambda bi, g, i: (bi, g, 0, 0)),
            pl.BlockSpec((1, 1, vr, s), lambda bi, g, i: (bi, g, 0, 0)),
        ],
        out_specs=pl.BlockSpec((1, tq, GQA_RATIO * HEAD_DIM), lambda bi, g, i: (bi, i, g)),
        out_shape=jax.ShapeDtypeStruct((b, s, n_q * HEAD_DIM), BF16),
        scratch_shapes=scratch,
        compiler_params=_params("parallel", "parallel", "parallel"),
        name=name,
    )(qT, k, vT)


def _attn_a(qT, k, vT, score_bound, tq=256, tk=256, tq_bounded=512, tk_bounded=1024):
    s = qT.shape[3]
    vr = vT.shape[2]
    tq_bounded, tk_bounded = min(tq_bounded, s), min(tk_bounded, s // 2)
    assert s % (2 * tk) == 0 and s % tq == 0 and s % (2 * tk_bounded) == 0 and s % tq_bounded == 0
    w, wb = GQA_RATIO * tq, GQA_RATIO * tq_bounded
    general = functools.partial(
        _attn_a_call, functools.partial(_attn_a_body, tk), tq,
        [pltpu.VMEM((HEAD_DIM, w), BF16), pltpu.VMEM((vr, w), F32),
         pltpu.VMEM((tk, w), F32), pltpu.VMEM((tk, w), F32), pltpu.VMEM((tk, w), BF16)], "attn_a")
    bounded = functools.partial(
        _attn_a_call, functools.partial(_attn_a_bounded_body, tk_bounded), tq_bounded,
        [pltpu.VMEM((HEAD_DIM, wb), BF16), pltpu.VMEM((tk_bounded, wb), BF16),
         pltpu.VMEM((tk_bounded, wb), BF16), pltpu.VMEM((vr, wb), F32)], "attn_a_bounded")
    return lax.cond(score_bound <= MAX_SAFE_LOG2_SCORE, bounded, general, qT, k, vT)


def _bproj_body(dilation, h_ref, w_ref, o_ref, scr):
    res = jnp.dot(h_ref[...], w_ref[...], preferred_element_type=F32)
    if dilation == 1:
        o_ref[0] = res.astype(o_ref.dtype)
    else:
        rows = scr.shape[1] // dilation
        for c in range(scr.shape[0]):
            scr[c] = res[:, c * 128:(c + 1) * 128]
        for r in range(dilation):
            for c in range(scr.shape[0]):
                o_ref[r, :, c * 128:(c + 1) * 128] = scr[c, pl.ds(r, rows, stride=dilation), :].astype(o_ref.dtype)


def _bproj(h, w, batch, dilation, tm=512):
    n, d = h.shape
    e = w.shape[1]
    s = n // batch
    per_b = s // tm
    assert tm % (16 * dilation) == 0
    return pl.pallas_call(
        functools.partial(_bproj_body, dilation),
        grid=(n // tm,),
        in_specs=[pl.BlockSpec((tm, d), lambda i: (i, 0)), pl.BlockSpec((d, e), lambda i: (0, 0))],
        out_specs=pl.BlockSpec((None, dilation, tm // dilation, e),
                               lambda i: (i // per_b, 0, i % per_b, 0)),
        out_shape=jax.ShapeDtypeStruct((batch, dilation, s // dilation, e), BF16),
        scratch_shapes=[pltpu.VMEM((e // 128, tm, 128), F32)],
        compiler_params=_params("parallel"),
        name=f"bproj_d{dilation}",
    )(h, w)


def _attn_b_body(dilation, radius, q_ref, kp_ref, kc_ref, kn_ref, vp_ref, vc_ref, vn_ref,
                 o_ref, lse_ref):
    tu = q_ref.shape[0]
    n_pairs = q_ref.shape[1] // 128
    i = pl.program_id(2)
    n_u = pl.num_programs(2) * tu
    row = lax.broadcasted_iota(I32, (tu, 2 * tu), 0)
    col = lax.broadcasted_iota(I32, (tu, 2 * tu), 1)
    j = col - radius - row
    key = i * tu - radius + col
    valid = (jnp.abs(j) <= radius) & (key >= 0) & (key < n_u)
    dist = (dilation * jnp.abs(j)).astype(F32)
    pen = jnp.where(valid, 0.0, NEG_BIG)
    first = lax.broadcasted_iota(I32, (1, 128), 1) < HEAD_DIM
    scale = HEAD_DIM ** -0.5
    n_heads = 2 * n_pairs
    for hp in range(n_pairs):
        lanes = slice(hp * 128, (hp + 1) * 128)
        q2 = q_ref[:, lanes]
        kw = jnp.concatenate([kp_ref[tu - radius:, lanes], kc_ref[:, lanes],
                              kn_ref[:tu - radius, lanes]], axis=0)
        vw = jnp.concatenate([vp_ref[tu - radius:, lanes], vc_ref[:, lanes],
                              vn_ref[:tu - radius, lanes]], axis=0)
        o2 = jnp.zeros((tu, 128), F32)
        lse2 = jnp.zeros((tu, 128), F32)
        for sub in range(2):
            sel = first if sub == 0 else jnp.logical_not(first)
            slope = 2.0 ** (-8.0 * (2 * hp + sub + 1) / n_heads)
            qm = jnp.where(sel, q2, jnp.zeros_like(q2))
            s = lax.dot_general(qm, kw, (((1,), (1,)), ((), ())), preferred_element_type=F32)
            s = s * scale + (pen - slope * dist)
            m = jnp.max(s, axis=1, keepdims=True)
            p = jnp.exp(s - m)
            l = jnp.sum(p, axis=1, keepdims=True)
            vm = jnp.where(sel, vw, jnp.zeros_like(vw))
            o2 = o2 + jnp.dot(p.astype(BF16), vm, preferred_element_type=F32) / l
            lse2 = jnp.where(sel, m + jnp.log(l), lse2)
        o_ref[:, lanes] = o2.astype(o_ref.dtype)
        lse_ref[:, lanes] = lse2


def _attn_b(pbd, window, tu=128):
    b, dilation, u, e = pbd.shape
    width = e // 3
    radius = window // (2 * dilation)
    assert radius < tu and u % tu == 0
    n_t = u // tu

    def spec(which, shift):
        def index(bi, r, i):
            return (bi, r, jnp.clip(i + shift, 0, n_t - 1), which)
        return pl.BlockSpec((None, None, tu, width), index)

    out_spec = pl.BlockSpec((None, None, tu, width), lambda bi, r, i: (bi, r, i, 0))
    return pl.pallas_call(
        functools.partial(_attn_b_body, dilation, radius),
        grid=(b, dilation, n_t),
        in_specs=[spec(0, 0), spec(1, -1), spec(1, 0), spec(1, 1), spec(2, -1), spec(2, 0), spec(2, 1)],
        out_specs=[out_spec, out_spec],
        out_shape=[jax.ShapeDtypeStruct((b, dilation, u, width), BF16),
                   jax.ShapeDtypeStruct((b, dilation, u, width), F32)],
        compiler_params=_params("parallel", "parallel", "parallel"),
        name=f"attn_b_d{dilation}",
    )(pbd, pbd, pbd, pbd, pbd, pbd, pbd)


def _rms(x, w):
    ms = jnp.mean(x * x, axis=-1, keepdims=True)
    return x * lax.rsqrt(ms + NORM_EPS) * w


def _to_slabs(ref, x):
    rows, width = x.shape
    c = width // 128
    for j in range(c):
        ref[pl.ds(j, rows, stride=c), :] = x[:, j * 128:(j + 1) * 128]


def _from_slabs(ref, rows, c):
    return jnp.concatenate([ref[pl.ds(j, rows, stride=c), :] for j in range(c)], axis=1)


def _token_order(ref, scr):
    d, rows, _ = ref.shape
    if d == 1:
        return ref[0].astype(F32)
    for r in range(d):
        slab = ref[r].astype(F32)
        for c in range(scr.shape[0]):
            scr[c, pl.ds(r, rows, stride=d), :] = slab[:, c * 128:(c + 1) * 128]
    return jnp.concatenate([scr[c] for c in range(scr.shape[0])], axis=1)


def _outproj_body(n_groups, *refs):
    oa_ref = refs[0]
    ob_refs = refs[1:1 + n_groups]
    lse_refs = refs[1 + n_groups:1 + 2 * n_groups]
    (x_ref, wo_a_ref, wo_b_ref, nwa_ref, nwb_ref, nm_ref, wr_ref, br_ref,
     x1_ref, hm_ref, lg_ref) = refs[1 + 2 * n_groups:12 + 2 * n_groups]
    scrs = refs[12 + 2 * n_groups:]
    obs = [_token_order(r, scrs[2 * g]) for g, r in enumerate(ob_refs)]
    lses = [_token_order(r, scrs[2 * g + 1]) for g, r in enumerate(lse_refs)]
    mx = functools.reduce(jnp.maximum, lses)
    ws = [jnp.exp(v - mx) for v in lses]
    den = functools.reduce(lambda a, c: a + c, ws)
    ob = functools.reduce(lambda a, c: a + c, [w * o for w, o in zip(ws, obs)]) / den
    na = _rms(oa_ref[...].astype(F32), nwa_ref[...]).astype(BF16)
    nb = _rms(ob, nwb_ref[...]).astype(BF16)
    x1 = (x_ref[...] + jnp.dot(na, wo_a_ref[...], preferred_element_type=F32)
          + jnp.dot(nb, wo_b_ref[...], preferred_element_type=F32))
    _to_slabs(x1_ref, x1)
    hm = _rms(x1, nm_ref[...])
    _to_slabs(hm_ref, hm)
    lg_ref[...] = jnp.dot(hm, wr_ref[...], precision=lax.Precision.HIGHEST,
                          preferred_element_type=F32) + br_ref[...]


def _outproj(o_a, o_bs, lses, x2, w_out, out_norm_w, norm_moe_w, w_router, b_router, tm=256):
    n, d = x2.shape
    wa, wb = o_a.shape[1], o_bs[0].shape[3]
    n_e = w_router.shape[1]
    per_b = n // o_bs[0].shape[0] // tm
    row = lambda width: pl.BlockSpec((tm, width), lambda i: (i, 0))
    full = lambda shape: pl.BlockSpec(shape, lambda i: (0,) * len(shape))
    grouped = [pl.BlockSpec((None, o.shape[1], tm // o.shape[1], wb),
                            lambda i: (i // per_b, 0, i % per_b, 0)) for o in o_bs]
    n_groups = len(o_bs)
    c = d // 128
    slabs = pl.BlockSpec((tm * c, 128), lambda i: (i, 0))
    return pl.pallas_call(
        functools.partial(_outproj_body, n_groups),
        grid=(n // tm,),
        in_specs=[row(wa)] + grouped + grouped + [
            row(d), full((wa, d)), full((wb, d)), full((1, wa)), full((1, wb)), full((1, d)),
            full((d, n_e)), full((1, n_e))],
        out_specs=[slabs, slabs, row(n_e)],
        out_shape=[jax.ShapeDtypeStruct((n * c, 128), F32), jax.ShapeDtypeStruct((n * c, 128), F32),
                   jax.ShapeDtypeStruct((n, n_e), F32)],
        scratch_shapes=[pltpu.VMEM((wb // 128, tm, 128), F32)] * (2 * n_groups),
        compiler_params=_params("parallel"),
        name="outproj",
    )(o_a, *o_bs, *lses, x2, w_out[:wa].astype(BF16), w_out[wa:].astype(BF16),
      out_norm_w[:wa].reshape(1, wa), out_norm_w[wa:].reshape(1, wb), norm_moe_w.reshape(1, d),
      w_router, b_router.reshape(1, n_e))


def _router_body(lg_ref, idx_ref, gate_ref, rank_ref, cnt_ref, carry_ref):
    i = pl.program_id(0)
    tm, n_e = lg_ref.shape

    @pl.when(i == 0)
    def _():
        carry_ref[...] = jnp.zeros_like(carry_ref)

    lane = lax.broadcasted_iota(I32, (tm, n_e), 1)
    work = lg_ref[...]
    vals, idxs = [], []
    for _ in range(TOP_K):
        mx = jnp.max(work, axis=1, keepdims=True)
        ix = jnp.min(jnp.where(work == mx, lane, n_e), axis=1, keepdims=True)
        vals.append(mx)
        idxs.append(ix)
        work = jnp.where(lane == ix, -jnp.inf, work)
    es = [jnp.exp(v - vals[0]) for v in vals]
    den = functools.reduce(lambda a, c: a + c, es)
    chosen = functools.reduce(lambda a, c: a + c, [(lane == ix).astype(F32) for ix in idxs])
    r = lax.broadcasted_iota(I32, (tm, tm), 0)
    c = lax.broadcasted_iota(I32, (tm, tm), 1)
    before = jnp.where(c < r, 1.0, 0.0).astype(BF16)
    prefix = jnp.dot(before, chosen.astype(BF16), preferred_element_type=F32) + carry_ref[...]
    for k in range(TOP_K):
        idx_ref[:, k:k + 1] = idxs[k]
        gate_ref[:, k:k + 1] = es[k] / den
        rank_ref[:, k:k + 1] = jnp.sum(jnp.where(lane == idxs[k], prefix, 0.0), axis=1,
                                       keepdims=True).astype(I32)
    carry_ref[...] += jnp.sum(chosen, axis=0, keepdims=True)
    cnt_ref[...] = carry_ref[...].astype(I32)


def _router(logits, tm=256):
    n, n_e = logits.shape
    row = pl.BlockSpec((tm, TOP_K), lambda i: (i, 0))
    return pl.pallas_call(
        _router_body,
        grid=(n // tm,),
        in_specs=[pl.BlockSpec((tm, n_e), lambda i: (i, 0))],
        out_specs=[row, row, row, pl.BlockSpec((1, n_e), lambda i: (0, 0))],
        out_shape=[jax.ShapeDtypeStruct((n, TOP_K), I32), jax.ShapeDtypeStruct((n, TOP_K), F32),
                   jax.ShapeDtypeStruct((n, TOP_K), I32), jax.ShapeDtypeStruct((1, n_e), I32)],
        scratch_shapes=[pltpu.VMEM((1, n_e), F32)],
        compiler_params=_params("arbitrary"),
        name="router",
    )(logits)


def _slab_copy(src_ref, row, buf_ref, slot, c, sem):
    return pltpu.make_async_copy(src_ref.at[pl.ds(pl.multiple_of(row * c, c), c)],
                                 buf_ref.at[pl.ds(pl.multiple_of(slot * c, c), c)], sem)


def _row_gather_body(rows_per_step, c, idx_ref, src_ref, dst_ref, buf_ref, sem):
    def issue(r, carry):
        _slab_copy(src_ref, idx_ref[0, 0, r], buf_ref, r, c, sem).start()
        return carry

    lax.fori_loop(0, rows_per_step, issue, 0, unroll=8)
    pltpu.make_async_copy(src_ref.at[pl.ds(0, rows_per_step * c)], buf_ref, sem).wait()
    dst_ref[...] = _from_slabs(buf_ref, rows_per_step, c).astype(dst_ref.dtype)


def _row_gather(src, idx, c, out_dtype, rows_per_step=1024):
    n_out = idx.shape[0]
    while n_out % rows_per_step:
        rows_per_step //= 2
    steps = n_out // rows_per_step
    return pl.pallas_call(
        functools.partial(_row_gather_body, rows_per_step, c),
        grid=(steps,),
        in_specs=[pl.BlockSpec((1, 1, rows_per_step), lambda i: (i, 0, 0), memory_space=pltpu.SMEM),
                  pl.BlockSpec(memory_space=pl.ANY)],
        out_specs=pl.BlockSpec((rows_per_step, c * 128), lambda i: (i, 0)),
        out_shape=jax.ShapeDtypeStruct((n_out, c * 128), out_dtype),
        scratch_shapes=[pltpu.VMEM((rows_per_step * c, 128), src.dtype), pltpu.SemaphoreType.DMA(())],
        compiler_params=_params("arbitrary"),
        name="dispatch",
    )(idx.reshape(steps, 1, rows_per_step), src)


def _ffn_body(blk_e_ref, n_used_ref, x_ref, wg_ref, wu_ref, bg_ref, bu_ref, wd_ref, bd_ref, g_ref,
              y_ref, acc_ref):
    i, f = pl.program_id(0), pl.program_id(1)
    n_f = pl.num_programs(1)
    used = i < n_used_ref[0]

    @pl.when(used)
    def _():
        x = x_ref[...]
        gate = jnp.dot(x, wg_ref[0], preferred_element_type=F32) + bg_ref[0]
        up = jnp.dot(x, wu_ref[0], preferred_element_type=F32) + bu_ref[0]
        gate = jnp.minimum(gate, SWIGLU_LIMIT)
        up = jnp.clip(up, -SWIGLU_LIMIT, SWIGLU_LIMIT)
        act = (up + 1.0) * gate * jax.nn.sigmoid(SWIGLU_ALPHA * gate)
        part = jnp.dot(act.astype(BF16), wd_ref[0], preferred_element_type=F32)

        @pl.when(f == 0)
        def _():
            acc_ref[...] = part

        @pl.when(f > 0)
        def _():
            acc_ref[...] += part

        @pl.when(f == n_f - 1)
        def _():
            _to_slabs(y_ref, (acc_ref[...] + bd_ref[0]) * g_ref[...])

    @pl.when(jnp.logical_not(used) & (f == n_f - 1))
    def _():
        y_ref[...] = jnp.zeros_like(y_ref)


def _expert_ffn(xs, blk_e, n_used, w_gu, b_gu, w_down, b_down, gate_buf, tf=512):
    cap, d = xs.shape
    n_e, _, two_f = w_gu.shape
    ff = two_f // 2
    n_f = ff // tf
    n_blocks = cap // MOE_BLOCK

    def fidx(i, f, nu):
        return jnp.where(i < nu[0], f, n_f - 1)

    grid_spec = pltpu.PrefetchScalarGridSpec(
        num_scalar_prefetch=2,
        grid=(n_blocks, n_f),
        in_specs=[
            pl.BlockSpec((MOE_BLOCK, d), lambda i, f, be, nu: (jnp.minimum(i, nu[0] - 1), 0)),
            pl.BlockSpec((1, d, tf), lambda i, f, be, nu: (be[i], 0, fidx(i, f, nu))),
            pl.BlockSpec((1, d, tf), lambda i, f, be, nu: (be[i], 0, n_f + fidx(i, f, nu))),
            pl.BlockSpec((1, 1, tf), lambda i, f, be, nu: (be[i], 0, fidx(i, f, nu))),
            pl.BlockSpec((1, 1, tf), lambda i, f, be, nu: (be[i], 0, n_f + fidx(i, f, nu))),
            pl.BlockSpec((1, tf, d), lambda i, f, be, nu: (be[i], fidx(i, f, nu), 0)),
            pl.BlockSpec((1, 1, d), lambda i, f, be, nu: (be[i], 0, 0)),
            pl.BlockSpec((MOE_BLOCK, 1), lambda i, f, be, nu: (i, 0)),
        ],
        out_specs=pl.BlockSpec((MOE_BLOCK * (d // 128), 128), lambda i, f, be, nu: (i, 0)),
        scratch_shapes=[pltpu.VMEM((MOE_BLOCK, d), F32)],
    )
    return pl.pallas_call(
        _ffn_body,
        grid_spec=grid_spec,
        out_shape=jax.ShapeDtypeStruct((cap * (d // 128), 128), F32),
        compiler_params=_params("arbitrary", "arbitrary"),
        name="expert_ffn",
    )(blk_e, n_used, xs, w_gu, w_gu, b_gu.reshape(n_e, 1, two_f), b_gu.reshape(n_e, 1, two_f),
      w_down, b_down.reshape(n_e, 1, d), gate_buf.reshape(cap, 1))


def _combine_body(tm, c, dest_ref, y_ref, x1_ref, w_ref, o_ref, buf_ref, sum_ref, sem):
    def issue(t, carry):
        for k in range(TOP_K):
            _slab_copy(y_ref, dest_ref[0, 0, t * TOP_K + k], buf_ref.at[k], t, c, sem).start()
        return carry

    lax.fori_loop(0, tm, issue, 0, unroll=2)
    for k in range(TOP_K):
        pltpu.make_async_copy(y_ref.at[pl.ds(0, tm * c)], buf_ref.at[k], sem).wait()
    x = x1_ref[...]
    for k in range(TOP_K):
        x = x + buf_ref[k]
    x = x.reshape(tm, c, 128)
    ms = jnp.sum(jnp.sum(x * x, axis=2, keepdims=True), axis=1, keepdims=True) / (c * 128)
    sum_ref[...] = (x * lax.rsqrt(ms + NORM_EPS)).reshape(tm * c, 128)
    o_ref[...] = _from_slabs(sum_ref, tm, c) * w_ref[...]


def _combine(y, dest, x1, w, c, tm=256):
    n = x1.shape[0] // c
    steps = n // tm
    return pl.pallas_call(
        functools.partial(_combine_body, tm, c),
        grid=(steps,),
        in_specs=[pl.BlockSpec((1, 1, tm * TOP_K), lambda i: (i, 0, 0), memory_space=pltpu.SMEM),
                  pl.BlockSpec(memory_space=pl.ANY),
                  pl.BlockSpec((tm * c, 128), lambda i: (i, 0)),
                  pl.BlockSpec((1, c * 128), lambda i: (0, 0))],
        out_specs=pl.BlockSpec((tm, c * 128), lambda i: (i, 0)),
        out_shape=jax.ShapeDtypeStruct((n, c * 128), F32),
        scratch_shapes=[pltpu.VMEM((TOP_K, tm * c, 128), F32), pltpu.VMEM((tm * c, 128), F32),
                        pltpu.SemaphoreType.DMA(())],
        compiler_params=_params("arbitrary"),
        name="combine",
    )(dest.reshape(steps, 1, tm * TOP_K), y, x1, w.reshape(1, c * 128))


def _layer(x, norm_mix_w, w_in, q_norm_w, k_norm_w, out_norm_w, w_out, norm_moe_w,
           w_router, b_router, w_gate_up, b_gate_up, w_down, b_down):
    b, s, d = x.shape
    n = b * s
    x2 = x.reshape(n, d)
    b_w = d // 4
    a_q_w = d - b_w
    a_kv_w = a_q_w // GQA_RATIO

    h = _prenorm(x2, norm_mix_w)
    qT, k, vT = _aproj(h.reshape(b, s, d), w_in[:, :a_q_w], w_in[:, a_q_w:a_q_w + a_kv_w],
                       w_in[:, a_q_w + a_kv_w:a_q_w + 2 * a_kv_w], q_norm_w, k_norm_w)
    score_bound = (1.02 * HEAD_DIM ** 0.5 * LOG2E) * jnp.max(jnp.abs(q_norm_w)) * jnp.max(jnp.abs(k_norm_w))
    o_a = _attn_a(qT, k, vT, score_bound).reshape(n, a_q_w)

    o_bs, lses = [], []
    base = a_q_w + 2 * a_kv_w
    for g, (window, dilation) in enumerate(B_CONFIGS):
        w_g = w_in[:, base + 3 * g * b_w:base + 3 * (g + 1) * b_w].astype(BF16)
        o_g, lse_g = _attn_b(_bproj(h, w_g, b, dilation), window)
        o_bs.append(o_g)
        lses.append(lse_g)

    x1, hm, logits = _outproj(o_a, o_bs, lses, x2, w_out, out_norm_w, norm_moe_w, w_router, b_router)

    top_idx, gates, rank, counts = _router(logits)
    counts = counts.reshape(N_EXPERTS)
    padded = (counts + MOE_BLOCK - 1) // MOE_BLOCK * MOE_BLOCK
    pend = jnp.cumsum(padded)
    pstart = pend - padded
    dest = pstart[top_idx] + rank
    n_blocks = -(-(n * TOP_K) // MOE_BLOCK) + N_EXPERTS
    cap = n_blocks * MOE_BLOCK
    flat_tok = jnp.repeat(jnp.arange(n, dtype=I32), TOP_K)
    tok_buf = jnp.zeros((cap,), I32).at[dest.reshape(-1)].set(flat_tok)
    gate_buf = jnp.zeros((cap,), F32).at[dest.reshape(-1)].set(gates.reshape(-1))
    blk_start = jnp.arange(n_blocks, dtype=I32) * MOE_BLOCK
    blk_e = jnp.minimum(jnp.sum((pend[None, :] <= blk_start[:, None]).astype(I32), axis=1), N_EXPERTS - 1)
    n_used = (pend[-1] // MOE_BLOCK).astype(I32).reshape(1)

    xs = _row_gather(hm, tok_buf, d // 128, BF16)
    y = _expert_ffn(xs, blk_e, n_used, w_gate_up.astype(BF16), b_gate_up, w_down.astype(BF16), b_down, gate_buf)
    return y, dest, x1


def kernel(x, norm_mix_w, w_in, q_norm_w, k_norm_w, out_norm_w, w_out, norm_moe_w, w_router, b_router,
           w_gate_up, b_gate_up, w_down, b_down, final_norm_w):
    b, s, d = x.shape
    assert w_in.shape[0] == 1, "the MoE combine is fused with the final norm: single-layer stacks only"
    y, dest, x1 = _layer(x, norm_mix_w[0], w_in[0], q_norm_w[0], k_norm_w[0], out_norm_w[0], w_out[0],
                         norm_moe_w[0], w_router[0], b_router[0], w_gate_up[0], b_gate_up[0],
                         w_down[0], b_down[0])
    return _combine(y, dest, x1, final_norm_w, d // 128).reshape(b, s, d)
```

```python
import functools

import jax
import jax.numpy as jnp
from jax import lax
from jax.experimental import pallas as pl
from jax.experimental.pallas import tpu as pltpu

F32 = jnp.float32
BF16 = jnp.bfloat16
I32 = jnp.int32

HEAD_DIM = 64
GQA_RATIO = 4
B_CONFIGS = ((128, 1), (512, 4), (2048, 16))
GRID_W = 64
ROPE_THETA = 10000.0
N_EXPERTS = 32
TOP_K = 4
MOE_BLOCK = 512
SWIGLU_LIMIT = 7.0
SWIGLU_ALPHA = 1.702
NORM_EPS = 1e-5
QK_EPS = 1e-6
LOG2E = 1.4426950408889634
NEG_BIG = -1e30
V_PAD_ROWS = 16
MAX_SAFE_LOG2_SCORE = 40.0
V7X_VMEM_LIMIT = 56 * 1024 * 1024


def _params(*sem):
    return pltpu.CompilerParams(dimension_semantics=sem, vmem_limit_bytes=V7X_VMEM_LIMIT)


def _prenorm_body(x_ref, w_ref, o_ref):
    x = x_ref[...]
    ms = jnp.mean(x * x, axis=-1, keepdims=True)
    o_ref[...] = (x * lax.rsqrt(ms + NORM_EPS) * w_ref[...]).astype(o_ref.dtype)


def _prenorm(x2, w, tm=512):
    n, d = x2.shape
    return pl.pallas_call(
        _prenorm_body,
        grid=(n // tm,),
        in_specs=[pl.BlockSpec((tm, d), lambda i: (i, 0)), pl.BlockSpec((1, d), lambda i: (0, 0))],
        out_specs=pl.BlockSpec((tm, d), lambda i: (i, 0)),
        out_shape=jax.ShapeDtypeStruct((n, d), BF16),
        compiler_params=_params("parallel"),
        name="prenorm",
    )(x2, w.reshape(1, d))


def _aproj_body(n_q, n_kv, h_ref, wqv_ref, wk_ref, qnw_ref, knw_ref, cos_ref, sin_ref,
                kc_ref, ksm_ref, ksp_ref, bd_ref, qT_ref, k_ref, vT_ref, pt_ref):
    h = h_ref[0]
    pt_ref[...] = lax.dot_general(wqv_ref[...], h, (((1,), (1,)), ((), ())),
                                  preferred_element_type=F32)
    q4 = HEAD_DIM // 4
    cr, cc = cos_ref[0:q4, :], cos_ref[q4:2 * q4, :]
    sr, sc = sin_ref[0:q4, :], sin_ref[q4:2 * q4, :]
    qnw = qnw_ref[...]

    def q_head(hh, carry):
        y = pt_ref[pl.ds(pl.multiple_of(hh * HEAD_DIM, HEAD_DIM), HEAD_DIM), :]
        ms = jnp.mean(y * y, axis=0, keepdims=True)
        y = y * lax.rsqrt(ms + QK_EPS) * qnw
        a1, a2, b1, b2 = y[0:q4], y[q4:2 * q4], y[2 * q4:3 * q4], y[3 * q4:]
        out = jnp.concatenate([a1 * cr - a2 * sr, a2 * cr + a1 * sr,
                               b1 * cc - b2 * sc, b2 * cc + b1 * sc], axis=0)
        qT_ref[0, hh] = (out * (HEAD_DIM ** -0.5 * LOG2E)).astype(BF16)
        return carry

    lax.fori_loop(0, n_q, q_head, 0)
    ones_row = (lax.broadcasted_iota(I32, (V_PAD_ROWS, pt_ref.shape[1]), 0) == 0).astype(BF16)
    for g in range(n_kv):
        lo = (n_q + g) * HEAD_DIM
        vT_ref[0, g, :HEAD_DIM] = pt_ref[lo:lo + HEAD_DIM, :].astype(BF16)
        vT_ref[0, g, HEAD_DIM:] = ones_row

    kn = jnp.dot(h, wk_ref[...], preferred_element_type=F32)
    ms = jnp.dot(kn * kn, bd_ref[...], precision=lax.Precision.HIGHEST,
                 preferred_element_type=F32)
    kn = kn * lax.rsqrt(ms + QK_EPS) * knw_ref[...]
    for c in range(n_kv // 2):
        y = kn[:, c * 128:(c + 1) * 128]
        out = (y * kc_ref[...] + pltpu.roll(y, 128 - q4, 1) * ksm_ref[...]
               + pltpu.roll(y, q4, 1) * ksp_ref[...])
        k_ref[0, 2 * c] = out[:, :HEAD_DIM].astype(BF16)
        k_ref[0, 2 * c + 1] = out[:, HEAD_DIM:].astype(BF16)


def _rope_tables(seq_len):
    rows = seq_len // GRID_W
    r, c = jnp.meshgrid(jnp.arange(rows), jnp.arange(GRID_W), indexing="ij")
    axis_dim = HEAD_DIM // 2
    inv = ROPE_THETA ** (-jnp.arange(0, axis_dim, 2, dtype=F32) / axis_dim)
    ang_r = r.reshape(-1).astype(F32)[:, None] * inv[None, :]
    ang_c = c.reshape(-1).astype(F32)[:, None] * inv[None, :]
    ang = jnp.concatenate([ang_r, ang_c], axis=-1)
    return jnp.cos(ang), jnp.sin(ang)


def _aproj(h3, w_q, w_k, w_v, q_norm_w, k_norm_w, ts=512):
    b, s, d = h3.shape
    n_q, n_kv = w_q.shape[1] // HEAD_DIM, w_k.shape[1] // HEAD_DIM
    q4 = HEAD_DIM // 4
    cos, sin = _rope_tables(s)
    zeros = jnp.zeros_like(sin[:, :q4])
    c64 = jnp.concatenate([cos[:, :q4], cos[:, :q4], cos[:, q4:], cos[:, q4:]], axis=1)
    sm64 = jnp.concatenate([-sin[:, :q4], zeros, -sin[:, q4:], zeros], axis=1)
    sp64 = jnp.concatenate([zeros, sin[:, :q4], zeros, sin[:, q4:]], axis=1)
    kc, ksm, ksp = (jnp.concatenate([t, t], axis=1) for t in (c64, sm64, sp64))
    head_of = jnp.arange(n_kv * HEAD_DIM) // HEAD_DIM
    bd = (head_of[:, None] == head_of[None, :]).astype(F32) / HEAD_DIM
    wqv_t = jnp.concatenate([w_q, w_v], axis=1).T.astype(BF16)
    rows = wqv_t.shape[0]
    full = lambda shape: pl.BlockSpec(shape, lambda bi, i: (0,) * len(shape))
    return pl.pallas_call(
        functools.partial(_aproj_body, n_q, n_kv),
        grid=(b, s // ts),
        in_specs=[
            pl.BlockSpec((1, ts, d), lambda bi, i: (bi, i, 0)),
            full((rows, d)), full((d, n_kv * HEAD_DIM)),
            full((HEAD_DIM, 1)), full((1, n_kv * HEAD_DIM)),
            pl.BlockSpec((2 * q4, ts), lambda bi, i: (0, i)),
            pl.BlockSpec((2 * q4, ts), lambda bi, i: (0, i)),
            pl.BlockSpec((ts, 128), lambda bi, i: (i, 0)),
            pl.BlockSpec((ts, 128), lambda bi, i: (i, 0)),
            pl.BlockSpec((ts, 128), lambda bi, i: (i, 0)),
            full((n_kv * HEAD_DIM, n_kv * HEAD_DIM)),
        ],
        out_specs=[
            pl.BlockSpec((1, n_q, HEAD_DIM, ts), lambda bi, i: (bi, 0, 0, i)),
            pl.BlockSpec((1, n_kv, ts, HEAD_DIM), lambda bi, i: (bi, 0, i, 0)),
            pl.BlockSpec((1, n_kv, HEAD_DIM + V_PAD_ROWS, ts), lambda bi, i: (bi, 0, 0, i)),
        ],
        out_shape=[
            jax.ShapeDtypeStruct((b, n_q, HEAD_DIM, s), BF16),
            jax.ShapeDtypeStruct((b, n_kv, s, HEAD_DIM), BF16),
            jax.ShapeDtypeStruct((b, n_kv, HEAD_DIM + V_PAD_ROWS, s), BF16),
        ],
        scratch_shapes=[pltpu.VMEM((rows, ts), F32)],
        compiler_params=_params("parallel", "parallel"),
        name="aproj",
    )(h3, wqv_t, w_k.astype(BF16), q_norm_w.reshape(HEAD_DIM, 1),
      jnp.tile(k_norm_w, n_kv).reshape(1, n_kv * HEAD_DIM), cos.T, sin.T, kc, ksm, ksp, bd)


def _attn_a_body(tk, qT_ref, k_ref, vT_ref, o_ref, q_scr, acc_scr, sa_scr, sb_scr, p_scr):
    s_len = k_ref.shape[2]
    tq = qT_ref.shape[3]
    w = GQA_RATIO * tq
    n_steps = s_len // tk
    rc = 32
    for hh in range(GQA_RATIO):
        q_scr[:, hh * tq:(hh + 1) * tq] = qT_ref[0, hh]
    acc_scr[...] = jnp.zeros(acc_scr.shape, F32)

    def scores(j, s_scr):
        off = pl.multiple_of(j * tk, tk)
        s_scr[...] = jnp.dot(k_ref[0, 0, pl.ds(off, tk), :], q_scr[...], preferred_element_type=F32)

    def half_step(j, m, s_scr, nxt_scr):
        scores(jnp.minimum(j + 1, n_steps - 1), nxt_scr)
        off = pl.multiple_of(j * tk, tk)
        v = vT_ref[0, 0, :, pl.ds(off, tk)]
        m8 = jnp.full((8, w), NEG_BIG, F32)
        for c in range(tk // rc):
            sc = s_scr[c * rc:(c + 1) * rc, :]
            m8 = jnp.maximum(m8, jnp.max(sc.reshape(rc // 8, 8, w), axis=0))
        m_new = jnp.maximum(m, jnp.max(m8, axis=0, keepdims=True))
        for c in range(tk // rc):
            sc = s_scr[c * rc:(c + 1) * rc, :]
            p_scr[c * rc:(c + 1) * rc, :] = jnp.exp2(sc - m_new).astype(BF16)
        alpha = jnp.exp2(m - m_new)
        acc_scr[...] = alpha * acc_scr[...] + jnp.dot(v, p_scr[...], preferred_element_type=F32)
        return m_new

    def kv_step(jj, m):
        m = half_step(2 * jj, m, sa_scr, sb_scr)
        return half_step(2 * jj + 1, m, sb_scr, sa_scr)

    scores(0, sa_scr)
    lax.fori_loop(0, n_steps // 2, kv_step, jnp.full((1, w), NEG_BIG, F32))
    o = acc_scr[:HEAD_DIM, :] / acc_scr[HEAD_DIM:HEAD_DIM + 1, :]
    for hh in range(GQA_RATIO):
        o_ref[0, :, hh * HEAD_DIM:(hh + 1) * HEAD_DIM] = o[:, hh * tq:(hh + 1) * tq].T.astype(o_ref.dtype)


def _attn_a_bounded_body(tk, qT_ref, k_ref, vT_ref, o_ref, q_scr, pa_scr, pb_scr, acc_scr):
    s_len = k_ref.shape[2]
    tq = qT_ref.shape[3]
    n_steps = s_len // tk
    for hh in range(GQA_RATIO):
        q_scr[:, hh * tq:(hh + 1) * tq] = qT_ref[0, hh]
    acc_scr[...] = jnp.zeros(acc_scr.shape, F32)

    def probs(j, p_scr):
        off = pl.multiple_of(j * tk, tk)
        s = jnp.dot(k_ref[0, 0, pl.ds(off, tk), :], q_scr[...], preferred_element_type=F32)
        p_scr[...] = jnp.exp2(s).astype(BF16)

    def half_step(j, p_scr, nxt_scr):
        probs(jnp.minimum(j + 1, n_steps - 1), nxt_scr)
        off = pl.multiple_of(j * tk, tk)
        acc_scr[...] += jnp.dot(vT_ref[0, 0, :, pl.ds(off, tk)], p_scr[...], preferred_element_type=F32)

    def kv_step(jj, carry):
        half_step(2 * jj, pa_scr, pb_scr)
        half_step(2 * jj + 1, pb_scr, pa_scr)
        return carry

    probs(0, pa_scr)
    lax.fori_loop(0, n_steps // 2, kv_step, 0)
    o = acc_scr[:HEAD_DIM, :] / acc_scr[HEAD_DIM:HEAD_DIM + 1, :]
    for hh in range(GQA_RATIO):
        o_ref[0, :, hh * HEAD_DIM:(hh + 1) * HEAD_DIM] = o[:, hh * tq:(hh + 1) * tq].T.astype(o_ref.dtype)


def _attn_a_call(body, tq, scratch, name, qT, k, vT):
    b, n_q, _, s = qT.shape
    n_kv, vr = k.shape[1], vT.shape[2]
    return pl.pallas_call(
        body,
        grid=(b, n_kv, s // tq),
        in_specs=[
            pl.BlockSpec((1, GQA_RATIO, HEAD_DIM, tq), lambda bi, g, i: (bi, g, 0, i)),
            pl.BlockSpec((1, 1, s, HEAD_DIM), lambda bi, g, i: (bi, g, 0, 0)),
            pl.BlockSpec((1, 1, vr, s), lambda bi, g, i: (bi, g, 0, 0)),
        ],
        out_specs=pl.BlockSpec((1, tq, GQA_RATIO * HEAD_DIM), lambda bi, g, i: (bi, i, g)),
        out_shape=jax.ShapeDtypeStruct((b, s, n_q * HEAD_DIM), BF16),
        scratch_shapes=scratch,
        compiler_params=_params("parallel", "parallel", "parallel"),
        name=name,
    )(qT, k, vT)


def _attn_a(qT, k, vT, score_bound, tq=256, tk=256, tq_bounded=512, tk_bounded=1024):
    s = qT.shape[3]
    vr = vT.shape[2]
    tq_bounded, tk_bounded = min(tq_bounded, s), min(tk_bounded, s // 2)
    assert s % (2 * tk) == 0 and s % tq == 0 and s % (2 * tk_bounded) == 0 and s % tq_bounded == 0
    w, wb = GQA_RATIO * tq, GQA_RATIO * tq_bounded
    general = functools.partial(
        _attn_a_call, functools.partial(_attn_a_body, tk), tq,
        [pltpu.VMEM((HEAD_DIM, w), BF16), pltpu.VMEM((vr, w), F32),
         pltpu.VMEM((tk, w), F32), pltpu.VMEM((tk, w), F32), pltpu.VMEM((tk, w), BF16)], "attn_a")
    bounded = functools.partial(
        _attn_a_call, functools.partial(_attn_a_bounded_body, tk_bounded), tq_bounded,
        [pltpu.VMEM((HEAD_DIM, wb), BF16), pltpu.VMEM((tk_bounded, wb), BF16),
         pltpu.VMEM((tk_bounded, wb), BF16), pltpu.VMEM((vr, wb), F32)], "attn_a_bounded")
    return lax.cond(score_bound <= MAX_SAFE_LOG2_SCORE, bounded, general, qT, k, vT)


def _bproj_body(dilation, h_ref, w_ref, o_ref, scr):
    res = jnp.dot(h_ref[...], w_ref[...], preferred_element_type=F32)
    if dilation == 1:
        o_ref[0] = res.astype(o_ref.dtype)
    else:
        rows = scr.shape[1] // dilation
        for c in range(scr.shape[0]):
            scr[c] = res[:, c * 128:(c + 1) * 128]
        for r in range(dilation):
            for c in range(scr.shape[0]):
                o_ref[r, :, c * 128:(c + 1) * 128] = scr[c, pl.ds(r, rows, stride=dilation), :].astype(o_ref.dtype)


def _bproj(h, w, batch, dilation, tm=512):
    n, d = h.shape
    e = w.shape[1]
    s = n // batch
    per_b = s // tm
    assert tm % (16 * dilation) == 0
    return pl.pallas_call(
        functools.partial(_bproj_body, dilation),
        grid=(n // tm,),
        in_specs=[pl.BlockSpec((tm, d), lambda i: (i, 0)), pl.BlockSpec((d, e), lambda i: (0, 0))],
        out_specs=pl.BlockSpec((None, dilation, tm // dilation, e),
                               lambda i: (i // per_b, 0, i % per_b, 0)),
        out_shape=jax.ShapeDtypeStruct((batch, dilation, s // dilation, e), BF16),
        scratch_shapes=[pltpu.VMEM((e // 128, tm, 128), F32)],
        compiler_params=_params("parallel"),
        name=f"bproj_d{dilation}",
    )(h, w)


def _attn_b_body(dilation, radius, q_ref, kp_ref, kc_ref, kn_ref, vp_ref, vc_ref, vn_ref,
                 o_ref, lse_ref):
    tu = q_ref.shape[0]
    n_pairs = q_ref.shape[1] // 128
    i = pl.program_id(2)
    n_u = pl.num_programs(2) * tu
    row = lax.broadcasted_iota(I32, (tu, 2 * tu), 0)
    col = lax.broadcasted_iota(I32, (tu, 2 * tu), 1)
    j = col - radius - row
    key = i * tu - radius + col
    valid = (jnp.abs(j) <= radius) & (key >= 0) & (key < n_u)
    dist = (dilation * jnp.abs(j)).astype(F32)
    pen = jnp.where(valid, 0.0, NEG_BIG)
    first = lax.broadcasted_iota(I32, (1, 128), 1) < HEAD_DIM
    scale = HEAD_DIM ** -0.5
    n_heads = 2 * n_pairs
    for hp in range(n_pairs):
        lanes = slice(hp * 128, (hp + 1) * 128)
        q2 = q_ref[:, lanes]
        kw = jnp.concatenate([kp_ref[tu - radius:, lanes], kc_ref[:, lanes],
                              kn_ref[:tu - radius, lanes]], axis=0)
        vw = jnp.concatenate([vp_ref[tu - radius:, lanes], vc_ref[:, lanes],
                              vn_ref[:tu - radius, lanes]], axis=0)
        o2 = jnp.zeros((tu, 128), F32)
        lse2 = jnp.zeros((tu, 128), F32)
        for sub in range(2):
            sel = first if sub == 0 else jnp.logical_not(first)
            slope = 2.0 ** (-8.0 * (2 * hp + sub + 1) / n_heads)
            qm = jnp.where(sel, q2, jnp.zeros_like(q2))
            s = lax.dot_general(qm, kw, (((1,), (1,)), ((), ())), preferred_element_type=F32)
            s = s * scale + (pen - slope * dist)
            m = jnp.max(s, axis=1, keepdims=True)
            p = jnp.exp(s - m)
            l = jnp.sum(p, axis=1, keepdims=True)
            vm = jnp.where(sel, vw, jnp.zeros_like(vw))
            o2 = o2 + jnp.dot(p.astype(BF16), vm, preferred_element_type=F32) / l
            lse2 = jnp.where(sel, m + jnp.log(l), lse2)
        o_ref[:, lanes] = o2.astype(o_ref.dtype)
        lse_ref[:, lanes] = lse2


def _attn_b(pbd, window, tu=128):
    b, dilation, u, e = pbd.shape
    width = e // 3
    radius = window // (2 * dilation)
    assert radius < tu and u % tu == 0
    n_t = u // tu

    def spec(which, shift):
        def index(bi, r, i):
            return (bi, r, jnp.clip(i + shift, 0, n_t - 1), which)
        return pl.BlockSpec((None, None, tu, width), index)

    out_spec = pl.BlockSpec((None, None, tu, width), lambda bi, r, i: (bi, r, i, 0))
    return pl.pallas_call(
        functools.partial(_attn_b_body, dilation, radius),
        grid=(b, dilation, n_t),
        in_specs=[spec(0, 0), spec(1, -1), spec(1, 0), spec(1, 1), spec(2, -1), spec(2, 0), spec(2, 1)],
        out_specs=[out_spec, out_spec],
        out_shape=[jax.ShapeDtypeStruct((b, dilation, u, width), BF16),
                   jax.ShapeDtypeStruct((b, dilation, u, width), F32)],
        compiler_params=_params("parallel", "parallel", "parallel"),
        name=f"attn_b_d{dilation}",
    )(pbd, pbd, pbd, pbd, pbd, pbd, pbd)


def _rms(x, w):
    ms = jnp.mean(x * x, axis=-1, keepdims=True)
    return x * lax.rsqrt(ms + NORM_EPS) * w


def _to_slabs(ref, x):
    rows, width = x.shape
    c = width // 128
    for j in range(c):
        ref[pl.ds(j, rows, stride=c), :] = x[:, j * 128:(j + 1) * 128]


def _from_slabs(ref, rows, c):
    return jnp.concatenate([ref[pl.ds(j, rows, stride=c), :] for j in range(c)], axis=1)


def _token_order(ref, scr):
    d, rows, _ = ref.shape
    if d == 1:
        return ref[0].astype(F32)
    for r in range(d):
        slab = ref[r].astype(F32)
        for c in range(scr.shape[0]):
            scr[c, pl.ds(r, rows, stride=d), :] = slab[:, c * 128:(c + 1) * 128]
    return jnp.concatenate([scr[c] for c in range(scr.shape[0])], axis=1)


def _outproj_body(n_groups, *refs):
    oa_ref = refs[0]
    ob_refs = refs[1:1 + n_groups]
    lse_refs = refs[1 + n_groups:1 + 2 * n_groups]
    (x_ref, wo_a_ref, wo_b_ref, nwa_ref, nwb_ref, nm_ref, wr_ref, br_ref,
     x1_ref, hm_ref, lg_ref) = refs[1 + 2 * n_groups:12 + 2 * n_groups]
    scrs = refs[12 + 2 * n_groups:]
    obs = [_token_order(r, scrs[2 * g]) for g, r in enumerate(ob_refs)]
    lses = [_token_order(r, scrs[2 * g + 1]) for g, r in enumerate(lse_refs)]
    mx = functools.reduce(jnp.maximum, lses)
    ws = [jnp.exp(v - mx) for v in lses]
    den = functools.reduce(lambda a, c: a + c, ws)
    ob = functools.reduce(lambda a, c: a + c, [w * o for w, o in zip(ws, obs)]) / den
    na = _rms(oa_ref[...].astype(F32), nwa_ref[...]).astype(BF16)
    nb = _rms(ob, nwb_ref[...]).astype(BF16)
    x1 = (x_ref[...] + jnp.dot(na, wo_a_ref[...], preferred_element_type=F32)
          + jnp.dot(nb, wo_b_ref[...], preferred_element_type=F32))
    _to_slabs(x1_ref, x1)
    hm = _rms(x1, nm_ref[...])
    _to_slabs(hm_ref, hm)
    lg_ref[...] = jnp.dot(hm, wr_ref[...], precision=lax.Precision.HIGHEST,
                          preferred_element_type=F32) + br_ref[...]


def _outproj(o_a, o_bs, lses, x2, w_out, out_norm_w, norm_moe_w, w_router, b_router, tm=256):
    n, d = x2.shape
    wa, wb = o_a.shape[1], o_bs[0].shape[3]
    n_e = w_router.shape[1]
    per_b = n // o_bs[0].shape[0] // tm
    row = lambda width: pl.BlockSpec((tm, width), lambda i: (i, 0))
    full = lambda shape: pl.BlockSpec(shape, lambda i: (0,) * len(shape))
    grouped = [pl.BlockSpec((None, o.shape[1], tm // o.shape[1], wb),
                            lambda i: (i // per_b, 0, i % per_b, 0)) for o in o_bs]
    n_groups = len(o_bs)
    c = d // 128
    slabs = pl.BlockSpec((tm * c, 128), lambda i: (i, 0))
    return pl.pallas_call(
        functools.partial(_outproj_body, n_groups),
        grid=(n // tm,),
        in_specs=[row(wa)] + grouped + grouped + [
            row(d), full((wa, d)), full((wb, d)), full((1, wa)), full((1, wb)), full((1, d)),
            full((d, n_e)), full((1, n_e))],
        out_specs=[slabs, slabs, row(n_e)],
        out_shape=[jax.ShapeDtypeStruct((n * c, 128), F32), jax.ShapeDtypeStruct((n * c, 128), F32),
                   jax.ShapeDtypeStruct((n, n_e), F32)],
        scratch_shapes=[pltpu.VMEM((wb // 128, tm, 128), F32)] * (2 * n_groups),
        compiler_params=_params("parallel"),
        name="outproj",
    )(o_a, *o_bs, *lses, x2, w_out[:wa].astype(BF16), w_out[wa:].astype(BF16),
      out_norm_w[:wa].reshape(1, wa), out_norm_w[wa:].reshape(1, wb), norm_moe_w.reshape(1, d),
      w_router, b_router.reshape(1, n_e))


def _router_body(lg_ref, idx_ref, gate_ref, rank_ref, cnt_ref, carry_ref):
    i = pl.program_id(0)
    tm, n_e = lg_ref.shape

    @pl.when(i == 0)
    def _():
        carry_ref[...] = jnp.zeros_like(carry_ref)

    lane = lax.broadcasted_iota(I32, (tm, n_e), 1)
    work = lg_ref[...]
    vals, idxs = [], []
    for _ in range(TOP_K):
        mx = jnp.max(work, axis=1, keepdims=True)
        ix = jnp.min(jnp.where(work == mx, lane, n_e), axis=1, keepdims=True)
        vals.append(mx)
        idxs.append(ix)
        work = jnp.where(lane == ix, -jnp.inf, work)
    es = [jnp.exp(v - vals[0]) for v in vals]
    den = functools.reduce(lambda a, c: a + c, es)
    chosen = functools.reduce(lambda a, c: a + c, [(lane == ix).astype(F32) for ix in idxs])
    r = lax.broadcasted_iota(I32, (tm, tm), 0)
    c = lax.broadcasted_iota(I32, (tm, tm), 1)
    before = jnp.where(c < r, 1.0, 0.0).astype(BF16)
    prefix = jnp.dot(before, chosen.astype(BF16), preferred_element_type=F32) + carry_ref[...]
    for k in range(TOP_K):
        idx_ref[:, k:k + 1] = idxs[k]
        gate_ref[:, k:k + 1] = es[k] / den
        rank_ref[:, k:k + 1] = jnp.sum(jnp.where(lane == idxs[k], prefix, 0.0), axis=1,
                                       keepdims=True).astype(I32)
    carry_ref[...] += jnp.sum(chosen, axis=0, keepdims=True)
    cnt_ref[...] = carry_ref[...].astype(I32)


def _router(logits, tm=256):
    n, n_e = logits.shape
    row = pl.BlockSpec((tm, TOP_K), lambda i: (i, 0))
    return pl.pallas_call(
        _router_body,
        grid=(n // tm,),
        in_specs=[pl.BlockSpec((tm, n_e), lambda i: (i, 0))],
        out_specs=[row, row, row, pl.BlockSpec((1, n_e), lambda i: (0, 0))],
        out_shape=[jax.ShapeDtypeStruct((n, TOP_K), I32), jax.ShapeDtypeStruct((n, TOP_K), F32),
                   jax.ShapeDtypeStruct((n, TOP_K), I32), jax.ShapeDtypeStruct((1, n_e), I32)],
        scratch_shapes=[pltpu.VMEM((1, n_e), F32)],
        compiler_params=_params("arbitrary"),
        name="router",
    )(logits)


def _slab_copy(src_ref, row, buf_ref, slot, c, sem):
    return pltpu.make_async_copy(src_ref.at[pl.ds(pl.multiple_of(row * c, c), c)],
                                 buf_ref.at[pl.ds(pl.multiple_of(slot * c, c), c)], sem)


def _two_slot_steps(issue, consume):
    i = pl.program_id(0)

    @pl.when(i == 0)
    def _():
        issue(0, 0)

    for slot in range(2):
        @pl.when(i % 2 == slot)
        def _():
            @pl.when(i + 1 < pl.num_programs(0))
            def _():
                issue(1, 1 - slot)

            consume(slot)


def _row_gather_body(rows_per_step, c, idx_ref, nxt_ref, src_ref, dst_ref, buf_ref, sem):
    def issue(step_offset, slot):
        ref = nxt_ref if step_offset else idx_ref

        def one(r, carry):
            _slab_copy(src_ref, ref[0, 0, r], buf_ref.at[slot], r, c, sem.at[slot]).start()
            return carry

        lax.fori_loop(0, rows_per_step, one, 0, unroll=8)

    def consume(slot):
        pltpu.make_async_copy(src_ref.at[pl.ds(0, rows_per_step * c)], buf_ref.at[slot], sem.at[slot]).wait()
        dst_ref[...] = _from_slabs(buf_ref.at[slot], rows_per_step, c).astype(dst_ref.dtype)

    _two_slot_steps(issue, consume)


def _row_gather(src, idx, c, out_dtype, rows_per_step=1024):
    n_out = idx.shape[0]
    while n_out % rows_per_step:
        rows_per_step //= 2
    steps = n_out // rows_per_step
    idx3 = idx.reshape(steps, 1, rows_per_step)
    return pl.pallas_call(
        functools.partial(_row_gather_body, rows_per_step, c),
        grid=(steps,),
        in_specs=[pl.BlockSpec((1, 1, rows_per_step), lambda i: (i, 0, 0), memory_space=pltpu.SMEM),
                  pl.BlockSpec((1, 1, rows_per_step), lambda i: (jnp.minimum(i + 1, steps - 1), 0, 0),
                               memory_space=pltpu.SMEM),
                  pl.BlockSpec(memory_space=pl.ANY)],
        out_specs=pl.BlockSpec((rows_per_step, c * 128), lambda i: (i, 0)),
        out_shape=jax.ShapeDtypeStruct((n_out, c * 128), out_dtype),
        scratch_shapes=[pltpu.VMEM((2, rows_per_step * c, 128), src.dtype), pltpu.SemaphoreType.DMA((2,))],
        compiler_params=_params("arbitrary"),
        name="dispatch",
    )(idx3, idx3, src)


def _ffn_body(blk_e_ref, n_used_ref, x_ref, wg_ref, wu_ref, bg_ref, bu_ref, wd_ref, bd_ref, g_ref,
              y_ref, act_ref):
    i, f = pl.program_id(0), pl.program_id(1)
    n_f = pl.num_programs(1)
    used = i < n_used_ref[0]

    @pl.when(used)
    def _():
        x = x_ref[...]
        tf = wg_ref.shape[2]
        gate = jnp.dot(x, wg_ref[0], preferred_element_type=F32) + bg_ref[0]
        up = jnp.dot(x, wu_ref[0], preferred_element_type=F32) + bu_ref[0]
        gate = jnp.minimum(gate, SWIGLU_LIMIT)
        up = jnp.clip(up, -SWIGLU_LIMIT, SWIGLU_LIMIT)
        act = (up + 1.0) * gate * jax.nn.sigmoid(SWIGLU_ALPHA * gate)
        act_ref[:, pl.ds(pl.multiple_of(f * tf, tf), tf)] = act.astype(BF16)

        @pl.when(f == n_f - 1)
        def _():
            y = jnp.dot(act_ref[...], wd_ref[0], preferred_element_type=F32)
            _to_slabs(y_ref, (y + bd_ref[0]) * g_ref[...])

    @pl.when(jnp.logical_not(used) & (f == n_f - 1))
    def _():
        y_ref[...] = jnp.zeros_like(y_ref)


def _expert_ffn(xs, blk_e, n_used, w_gu, b_gu, w_down, b_down, gate_buf, tf=512):
    cap, d = xs.shape
    n_e, _, two_f = w_gu.shape
    ff = two_f // 2
    n_f = ff // tf
    n_blocks = cap // MOE_BLOCK

    def fidx(i, f, nu):
        return jnp.where(i < nu[0], f, n_f - 1)

    grid_spec = pltpu.PrefetchScalarGridSpec(
        num_scalar_prefetch=2,
        grid=(n_blocks, n_f),
        in_specs=[
            pl.BlockSpec((MOE_BLOCK, d), lambda i, f, be, nu: (jnp.minimum(i, nu[0] - 1), 0)),
            pl.BlockSpec((1, d, tf), lambda i, f, be, nu: (be[i], 0, fidx(i, f, nu))),
            pl.BlockSpec((1, d, tf), lambda i, f, be, nu: (be[i], 0, n_f + fidx(i, f, nu))),
            pl.BlockSpec((1, 1, tf), lambda i, f, be, nu: (be[i], 0, fidx(i, f, nu))),
            pl.BlockSpec((1, 1, tf), lambda i, f, be, nu: (be[i], 0, n_f + fidx(i, f, nu))),
            pl.BlockSpec((1, ff, d), lambda i, f, be, nu: (be[i], 0, 0)),
            pl.BlockSpec((1, 1, d), lambda i, f, be, nu: (be[i], 0, 0)),
            pl.BlockSpec((MOE_BLOCK, 1), lambda i, f, be, nu: (i, 0)),
        ],
        out_specs=pl.BlockSpec((MOE_BLOCK * (d // 128), 128), lambda i, f, be, nu: (i, 0)),
        scratch_shapes=[pltpu.VMEM((MOE_BLOCK, ff), BF16)],
    )
    return pl.pallas_call(
        _ffn_body,
        grid_spec=grid_spec,
        out_shape=jax.ShapeDtypeStruct((cap * (d // 128), 128), F32),
        compiler_params=_params("arbitrary", "arbitrary"),
        name="expert_ffn",
    )(blk_e, n_used, xs, w_gu, w_gu, b_gu.reshape(n_e, 1, two_f), b_gu.reshape(n_e, 1, two_f),
      w_down, b_down.reshape(n_e, 1, d), gate_buf.reshape(cap, 1))


def _combine_body(tm, c, dest_ref, nxt_ref, y_ref, x1_ref, w_ref, o_ref, buf_ref, sum_ref, sem):
    def issue(step_offset, slot):
        ref = nxt_ref if step_offset else dest_ref

        def one(t, carry):
            for k in range(TOP_K):
                _slab_copy(y_ref, ref[0, 0, t * TOP_K + k], buf_ref.at[slot, k], t, c, sem.at[slot]).start()
            return carry

        lax.fori_loop(0, tm, one, 0, unroll=2)

    def consume(slot):
        for k in range(TOP_K):
            pltpu.make_async_copy(y_ref.at[pl.ds(0, tm * c)], buf_ref.at[slot, k], sem.at[slot]).wait()
        x = x1_ref[...]
        for k in range(TOP_K):
            x = x + buf_ref[slot, k]
        x = x.reshape(tm, c, 128)
        ms = jnp.sum(jnp.sum(x * x, axis=2, keepdims=True), axis=1, keepdims=True) / (c * 128)
        sum_ref[...] = (x * lax.rsqrt(ms + NORM_EPS)).reshape(tm * c, 128)
        o_ref[...] = _from_slabs(sum_ref, tm, c) * w_ref[...]

    _two_slot_steps(issue, consume)


def _combine(y, dest, x1, w, c, tm=256):
    n = x1.shape[0] // c
    steps = n // tm
    dest3 = dest.reshape(steps, 1, tm * TOP_K)
    return pl.pallas_call(
        functools.partial(_combine_body, tm, c),
        grid=(steps,),
        in_specs=[pl.BlockSpec((1, 1, tm * TOP_K), lambda i: (i, 0, 0), memory_space=pltpu.SMEM),
                  pl.BlockSpec((1, 1, tm * TOP_K), lambda i: (jnp.minimum(i + 1, steps - 1), 0, 0),
                               memory_space=pltpu.SMEM),
                  pl.BlockSpec(memory_space=pl.ANY),
                  pl.BlockSpec((tm * c, 128), lambda i: (i, 0)),
                  pl.BlockSpec((1, c * 128), lambda i: (0, 0))],
        out_specs=pl.BlockSpec((tm, c * 128), lambda i: (i, 0)),
        out_shape=jax.ShapeDtypeStruct((n, c * 128), F32),
        scratch_shapes=[pltpu.VMEM((2, TOP_K, tm * c, 128), F32), pltpu.VMEM((tm * c, 128), F32),
                        pltpu.SemaphoreType.DMA((2,))],
        compiler_params=_params("arbitrary"),
        name="combine",
    )(dest3, dest3, y, x1, w.reshape(1, c * 128))


def _layer(x, norm_mix_w, w_in, q_norm_w, k_norm_w, out_norm_w, w_out, norm_moe_w,
           w_router, b_router, w_gate_up, b_gate_up, w_down, b_down):
    b, s, d = x.shape
    n = b * s
    x2 = x.reshape(n, d)
    b_w = d // 4
    a_q_w = d - b_w
    a_kv_w = a_q_w // GQA_RATIO

    h = _prenorm(x2, norm_mix_w)
    qT, k, vT = _aproj(h.reshape(b, s, d), w_in[:, :a_q_w], w_in[:, a_q_w:a_q_w + a_kv_w],
                       w_in[:, a_q_w + a_kv_w:a_q_w + 2 * a_kv_w], q_norm_w, k_norm_w)
    score_bound = (1.02 * HEAD_DIM ** 0.5 * LOG2E) * jnp.max(jnp.abs(q_norm_w)) * jnp.max(jnp.abs(k_norm_w))
    o_a = _attn_a(qT, k, vT, score_bound).reshape(n, a_q_w)

    o_bs, lses = [], []
    base = a_q_w + 2 * a_kv_w
    for g, (window, dilation) in enumerate(B_CONFIGS):
        w_g = w_in[:, base + 3 * g * b_w:base + 3 * (g + 1) * b_w].astype(BF16)
        o_g, lse_g = _attn_b(_bproj(h, w_g, b, dilation), window)
        o_bs.append(o_g)
        lses.append(lse_g)

    x1, hm, logits = _outproj(o_a, o_bs, lses, x2, w_out, out_norm_w, norm_moe_w, w_router, b_router)

    top_idx, gates, rank, counts = _router(logits)
    counts = counts.reshape(N_EXPERTS)
    padded = (counts + MOE_BLOCK - 1) // MOE_BLOCK * MOE_BLOCK
    pend = jnp.cumsum(padded)
    pstart = pend - padded
    dest = pstart[top_idx] + rank
    n_blocks = -(-(n * TOP_K) // MOE_BLOCK) + N_EXPERTS
    cap = n_blocks * MOE_BLOCK
    flat_tok = jnp.repeat(jnp.arange(n, dtype=I32), TOP_K)
    packed = jnp.stack([flat_tok, lax.bitcast_convert_type(gates.reshape(-1), I32)], axis=1)
    packed = jnp.zeros((cap, 2), I32).at[dest.reshape(-1)].set(packed)
    tok_buf = packed[:, 0]
    gate_buf = lax.bitcast_convert_type(packed[:, 1], F32)
    blk_start = jnp.arange(n_blocks, dtype=I32) * MOE_BLOCK
    blk_e = jnp.minimum(jnp.sum((pend[None, :] <= blk_start[:, None]).astype(I32), axis=1), N_EXPERTS - 1)
    n_used = (pend[-1] // MOE_BLOCK).astype(I32).reshape(1)

    xs = _row_gather(hm, tok_buf, d // 128, BF16)
    y = _expert_ffn(xs, blk_e, n_used, w_gate_up.astype(BF16), b_gate_up, w_down.astype(BF16), b_down, gate_buf)
    return y, dest, x1


def kernel(x, norm_mix_w, w_in, q_norm_w, k_norm_w, out_norm_w, w_out, norm_moe_w, w_router, b_router,
           w_gate_up, b_gate_up, w_down, b_down, final_norm_w):
    b, s, d = x.shape
    assert w_in.shape[0] == 1, "the MoE combine is fused with the final norm: single-layer stacks only"
    y, dest, x1 = _layer(x, norm_mix_w[0], w_in[0], q_norm_w[0], k_norm_w[0], out_norm_w[0], w_out[0],
                         norm_moe_w[0], w_router[0], b_router[0], w_gate_up[0], b_gate_up[0],
                         w_down[0], b_down[0])
    return _combine(y, dest, x1, final_norm_w, d // 128).reshape(b, s, d)
```

```python
import functools

import jax
import jax.numpy as jnp
from jax import lax
from jax.experimental import pallas as pl
from jax.experimental.pallas import tpu as pltpu

F32 = jnp.float32
BF16 = jnp.bfloat16
I32 = jnp.int32

HEAD_DIM = 64
GQA_RATIO = 4
B_CONFIGS = ((128, 1), (512, 4), (2048, 16))
GRID_W = 64
ROPE_THETA = 10000.0
N_EXPERTS = 32
TOP_K = 4
MOE_BLOCK = 512
SWIGLU_LIMIT = 7.0
SWIGLU_ALPHA = 1.702
NORM_EPS = 1e-5
QK_EPS = 1e-6
LOG2E = 1.4426950408889634
NEG_BIG = -1e30
V_PAD_ROWS = 16
MAX_SAFE_LOG2_SCORE = 40.0
V7X_VMEM_LIMIT = 56 * 1024 * 1024


def _params(*sem):
    return pltpu.CompilerParams(dimension_semantics=sem, vmem_limit_bytes=V7X_VMEM_LIMIT)


def _prenorm_body(x_ref, w_ref, o_ref):
    x = x_ref[...]
    ms = jnp.mean(x * x, axis=-1, keepdims=True)
    o_ref[...] = (x * lax.rsqrt(ms + NORM_EPS) * w_ref[...]).astype(o_ref.dtype)


def _prenorm(x2, w, tm=512):
    n, d = x2.shape
    return pl.pallas_call(
        _prenorm_body,
        grid=(n // tm,),
        in_specs=[pl.BlockSpec((tm, d), lambda i: (i, 0)), pl.BlockSpec((1, d), lambda i: (0, 0))],
        out_specs=pl.BlockSpec((tm, d), lambda i: (i, 0)),
        out_shape=jax.ShapeDtypeStruct((n, d), BF16),
        compiler_params=_params("parallel"),
        name="prenorm",
    )(x2, w.reshape(1, d))


def _aproj_body(n_q, n_kv, h_ref, wqv_ref, wk_ref, qnw_ref, knw_ref, cos_ref, sin_ref,
                kc_ref, ksm_ref, ksp_ref, bd_ref, qT_ref, k_ref, vT_ref, pt_ref):
    h = h_ref[0]
    pt_ref[...] = lax.dot_general(wqv_ref[...], h, (((1,), (1,)), ((), ())),
                                  preferred_element_type=F32)
    q4 = HEAD_DIM // 4
    cr, cc = cos_ref[0:q4, :], cos_ref[q4:2 * q4, :]
    sr, sc = sin_ref[0:q4, :], sin_ref[q4:2 * q4, :]
    qnw = qnw_ref[...]

    def q_head(hh, carry):
        y = pt_ref[pl.ds(pl.multiple_of(hh * HEAD_DIM, HEAD_DIM), HEAD_DIM), :]
        ms = jnp.mean(y * y, axis=0, keepdims=True)
        y = y * lax.rsqrt(ms + QK_EPS) * qnw
        a1, a2, b1, b2 = y[0:q4], y[q4:2 * q4], y[2 * q4:3 * q4], y[3 * q4:]
        out = jnp.concatenate([a1 * cr - a2 * sr, a2 * cr + a1 * sr,
                               b1 * cc - b2 * sc, b2 * cc + b1 * sc], axis=0)
        qT_ref[0, hh] = (out * (HEAD_DIM ** -0.5 * LOG2E)).astype(BF16)
        return carry

    lax.fori_loop(0, n_q, q_head, 0)
    ones_row = (lax.broadcasted_iota(I32, (V_PAD_ROWS, pt_ref.shape[1]), 0) == 0).astype(BF16)
    for g in range(n_kv):
        lo = (n_q + g) * HEAD_DIM
        vT_ref[0, g, :HEAD_DIM] = pt_ref[lo:lo + HEAD_DIM, :].astype(BF16)
        vT_ref[0, g, HEAD_DIM:] = ones_row

    kn = jnp.dot(h, wk_ref[...], preferred_element_type=F32)
    ms = jnp.dot(kn * kn, bd_ref[...], precision=lax.Precision.HIGHEST,
                 preferred_element_type=F32)
    kn = kn * lax.rsqrt(ms + QK_EPS) * knw_ref[...]
    for c in range(n_kv // 2):
        y = kn[:, c * 128:(c + 1) * 128]
        out = (y * kc_ref[...] + pltpu.roll(y, 128 - q4, 1) * ksm_ref[...]
               + pltpu.roll(y, q4, 1) * ksp_ref[...])
        k_ref[0, 2 * c] = out[:, :HEAD_DIM].astype(BF16)
        k_ref[0, 2 * c + 1] = out[:, HEAD_DIM:].astype(BF16)


def _rope_tables(seq_len):
    rows = seq_len // GRID_W
    r, c = jnp.meshgrid(jnp.arange(rows), jnp.arange(GRID_W), indexing="ij")
    axis_dim = HEAD_DIM // 2
    inv = ROPE_THETA ** (-jnp.arange(0, axis_dim, 2, dtype=F32) / axis_dim)
    ang_r = r.reshape(-1).astype(F32)[:, None] * inv[None, :]
    ang_c = c.reshape(-1).astype(F32)[:, None] * inv[None, :]
    ang = jnp.concatenate([ang_r, ang_c], axis=-1)
    return jnp.cos(ang), jnp.sin(ang)


def _aproj(h3, w_q, w_k, w_v, q_norm_w, k_norm_w, ts=512):
    b, s, d = h3.shape
    n_q, n_kv = w_q.shape[1] // HEAD_DIM, w_k.shape[1] // HEAD_DIM
    q4 = HEAD_DIM // 4
    cos, sin = _rope_tables(s)
    zeros = jnp.zeros_like(sin[:, :q4])
    c64 = jnp.concatenate([cos[:, :q4], cos[:, :q4], cos[:, q4:], cos[:, q4:]], axis=1)
    sm64 = jnp.concatenate([-sin[:, :q4], zeros, -sin[:, q4:], zeros], axis=1)
    sp64 = jnp.concatenate([zeros, sin[:, :q4], zeros, sin[:, q4:]], axis=1)
    kc, ksm, ksp = (jnp.concatenate([t, t], axis=1) for t in (c64, sm64, sp64))
    head_of = jnp.arange(n_kv * HEAD_DIM) // HEAD_DIM
    bd = (head_of[:, None] == head_of[None, :]).astype(F32) / HEAD_DIM
    wqv_t = jnp.concatenate([w_q, w_v], axis=1).T.astype(BF16)
    rows = wqv_t.shape[0]
    full = lambda shape: pl.BlockSpec(shape, lambda bi, i: (0,) * len(shape))
    return pl.pallas_call(
        functools.partial(_aproj_body, n_q, n_kv),
        grid=(b, s // ts),
        in_specs=[
            pl.BlockSpec((1, ts, d), lambda bi, i: (bi, i, 0)),
            full((rows, d)), full((d, n_kv * HEAD_DIM)),
            full((HEAD_DIM, 1)), full((1, n_kv * HEAD_DIM)),
            pl.BlockSpec((2 * q4, ts), lambda bi, i: (0, i)),
            pl.BlockSpec((2 * q4, ts), lambda bi, i: (0, i)),
            pl.BlockSpec((ts, 128), lambda bi, i: (i, 0)),
            pl.BlockSpec((ts, 128), lambda bi, i: (i, 0)),
            pl.BlockSpec((ts, 128), lambda bi, i: (i, 0)),
            full((n_kv * HEAD_DIM, n_kv * HEAD_DIM)),
        ],
        out_specs=[
            pl.BlockSpec((1, n_q, HEAD_DIM, ts), lambda bi, i: (bi, 0, 0, i)),
            pl.BlockSpec((1, n_kv, ts, HEAD_DIM), lambda bi, i: (bi, 0, i, 0)),
            pl.BlockSpec((1, n_kv, HEAD_DIM + V_PAD_ROWS, ts), lambda bi, i: (bi, 0, 0, i)),
        ],
        out_shape=[
            jax.ShapeDtypeStruct((b, n_q, HEAD_DIM, s), BF16),
            jax.ShapeDtypeStruct((b, n_kv, s, HEAD_DIM), BF16),
            jax.ShapeDtypeStruct((b, n_kv, HEAD_DIM + V_PAD_ROWS, s), BF16),
        ],
        scratch_shapes=[pltpu.VMEM((rows, ts), F32)],
        compiler_params=_params("parallel", "parallel"),
        name="aproj",
    )(h3, wqv_t, w_k.astype(BF16), q_norm_w.reshape(HEAD_DIM, 1),
      jnp.tile(k_norm_w, n_kv).reshape(1, n_kv * HEAD_DIM), cos.T, sin.T, kc, ksm, ksp, bd)


def _attn_a_body(tk, qT_ref, k_ref, vT_ref, o_ref, q_scr, acc_scr, sa_scr, sb_scr, p_scr):
    s_len = k_ref.shape[2]
    tq = qT_ref.shape[3]
    w = GQA_RATIO * tq
    n_steps = s_len // tk
    rc = 32
    for hh in range(GQA_RATIO):
        q_scr[:, hh * tq:(hh + 1) * tq] = qT_ref[0, hh]
    acc_scr[...] = jnp.zeros(acc_scr.shape, F32)

    def scores(j, s_scr):
        off = pl.multiple_of(j * tk, tk)
        s_scr[...] = jnp.dot(k_ref[0, 0, pl.ds(off, tk), :], q_scr[...], preferred_element_type=F32)

    def half_step(j, m, s_scr, nxt_scr):
        scores(jnp.minimum(j + 1, n_steps - 1), nxt_scr)
        off = pl.multiple_of(j * tk, tk)
        v = vT_ref[0, 0, :, pl.ds(off, tk)]
        m8 = jnp.full((8, w), NEG_BIG, F32)
        for c in range(tk // rc):
            sc = s_scr[c * rc:(c + 1) * rc, :]
            m8 = jnp.maximum(m8, jnp.max(sc.reshape(rc // 8, 8, w), axis=0))
        m_new = jnp.maximum(m, jnp.max(m8, axis=0, keepdims=True))
        for c in range(tk // rc):
            sc = s_scr[c * rc:(c + 1) * rc, :]
            p_scr[c * rc:(c + 1) * rc, :] = jnp.exp2(sc - m_new).astype(BF16)
        alpha = jnp.exp2(m - m_new)
        acc_scr[...] = alpha * acc_scr[...] + jnp.dot(v, p_scr[...], preferred_element_type=F32)
        return m_new

    def kv_step(jj, m):
        m = half_step(2 * jj, m, sa_scr, sb_scr)
        return half_step(2 * jj + 1, m, sb_scr, sa_scr)

    scores(0, sa_scr)
    lax.fori_loop(0, n_steps // 2, kv_step, jnp.full((1, w), NEG_BIG, F32))
    o = acc_scr[:HEAD_DIM, :] / acc_scr[HEAD_DIM:HEAD_DIM + 1, :]
    for hh in range(GQA_RATIO):
        o_ref[0, :, hh * HEAD_DIM:(hh + 1) * HEAD_DIM] = o[:, hh * tq:(hh + 1) * tq].T.astype(o_ref.dtype)


def _attn_a_bounded_body(tk, qT_ref, k_ref, vT_ref, o_ref, q_scr, pa_scr, pb_scr, acc_scr):
    s_len = k_ref.shape[2]
    tq = qT_ref.shape[3]
    n_steps = s_len // tk
    for hh in range(GQA_RATIO):
        q_scr[:, hh * tq:(hh + 1) * tq] = qT_ref[0, hh]
    acc_scr[...] = jnp.zeros(acc_scr.shape, F32)

    def chunk(j):
        return pl.ds(j * tk if isinstance(j, int) else pl.multiple_of(j * tk, tk), tk)

    def probs(j, p_scr):
        s = jnp.dot(k_ref[0, 0, chunk(j), :], q_scr[...], preferred_element_type=F32)
        p_scr[...] = jnp.exp2(s).astype(BF16)

    def half_step(j, p_scr, nxt_scr):
        if nxt_scr is not None:
            probs(j + 1, nxt_scr)
        acc_scr[...] += jnp.dot(vT_ref[0, 0, :, chunk(j)], p_scr[...], preferred_element_type=F32)

    def kv_step(jj, carry):
        half_step(2 * jj, pa_scr, pb_scr)
        half_step(2 * jj + 1, pb_scr, pa_scr)
        return carry

    probs(0, pa_scr)
    lax.fori_loop(0, n_steps // 2 - 1, kv_step, 0)
    half_step(n_steps - 2, pa_scr, pb_scr)
    half_step(n_steps - 1, pb_scr, None)
    o = acc_scr[:HEAD_DIM, :] / acc_scr[HEAD_DIM:HEAD_DIM + 1, :]
    for hh in range(GQA_RATIO):
        o_ref[0, :, hh * HEAD_DIM:(hh + 1) * HEAD_DIM] = o[:, hh * tq:(hh + 1) * tq].T.astype(o_ref.dtype)


def _attn_a_call(body, tq, scratch, name, qT, k, vT):
    b, n_q, _, s = qT.shape
    n_kv, vr = k.shape[1], vT.shape[2]
    return pl.pallas_call(
        body,
        grid=(b, n_kv, s // tq),
        in_specs=[
            pl.BlockSpec((1, GQA_RATIO, HEAD_DIM, tq), lambda bi, g, i: (bi, g, 0, i)),
            pl.BlockSpec((1, 1, s, HEAD_DIM), lambda bi, g, i: (bi, g, 0, 0)),
            pl.BlockSpec((1, 1, vr, s), lambda bi, g, i: (bi, g, 0, 0)),
        ],
        out_specs=pl.BlockSpec((1, tq, GQA_RATIO * HEAD_DIM), lambda bi, g, i: (bi, i, g)),
        out_shape=jax.ShapeDtypeStruct((b, s, n_q * HEAD_DIM), BF16),
        scratch_shapes=scratch,
        compiler_params=_params("parallel", "parallel", "parallel"),
        name=name,
    )(qT, k, vT)


def _attn_a(qT, k, vT, score_bound, tq=256, tk=256, tq_bounded=512, tk_bounded=1024):
    s = qT.shape[3]
    vr = vT.shape[2]
    tq_bounded, tk_bounded = min(tq_bounded, s), min(tk_bounded, s // 2)
    assert s % (2 * tk) == 0 and s % tq == 0 and s % (2 * tk_bounded) == 0 and s % tq_bounded == 0
    w, wb = GQA_RATIO * tq, GQA_RATIO * tq_bounded
    general = functools.partial(
        _attn_a_call, functools.partial(_attn_a_body, tk), tq,
        [pltpu.VMEM((HEAD_DIM, w), BF16), pltpu.VMEM((vr, w), F32),
         pltpu.VMEM((tk, w), F32), pltpu.VMEM((tk, w), F32), pltpu.VMEM((tk, w), BF16)], "attn_a")
    bounded = functools.partial(
        _attn_a_call, functools.partial(_attn_a_bounded_body, tk_bounded), tq_bounded,
        [pltpu.VMEM((HEAD_DIM, wb), BF16), pltpu.VMEM((tk_bounded, wb), BF16),
         pltpu.VMEM((tk_bounded, wb), BF16), pltpu.VMEM((vr, wb), F32)], "attn_a_bounded")
    return lax.cond(score_bound <= MAX_SAFE_LOG2_SCORE, bounded, general, qT, k, vT)


def _bproj_body(dilation, h_ref, w_ref, o_ref, scr):
    res = jnp.dot(h_ref[...], w_ref[...], preferred_element_type=F32)
    if dilation == 1:
        o_ref[0] = res.astype(o_ref.dtype)
    else:
        rows = scr.shape[1] // dilation
        for c in range(scr.shape[0]):
            scr[c] = res[:, c * 128:(c + 1) * 128]
        for r in range(dilation):
            for c in range(scr.shape[0]):
                o_ref[r, :, c * 128:(c + 1) * 128] = scr[c, pl.ds(r, rows, stride=dilation), :].astype(o_ref.dtype)


def _bproj(h, w, batch, dilation, tm=512):
    n, d = h.shape
    e = w.shape[1]
    s = n // batch
    per_b = s // tm
    assert tm % (16 * dilation) == 0
    return pl.pallas_call(
        functools.partial(_bproj_body, dilation),
        grid=(n // tm,),
        in_specs=[pl.BlockSpec((tm, d), lambda i: (i, 0)), pl.BlockSpec((d, e), lambda i: (0, 0))],
        out_specs=pl.BlockSpec((None, dilation, tm // dilation, e),
                               lambda i: (i // per_b, 0, i % per_b, 0)),
        out_shape=jax.ShapeDtypeStruct((batch, dilation, s // dilation, e), BF16),
        scratch_shapes=[pltpu.VMEM((e // 128, tm, 128), F32)],
        compiler_params=_params("parallel"),
        name=f"bproj_d{dilation}",
    )(h, w)


def _attn_b_body(dilation, radius, q_ref, kp_ref, kc_ref, kn_ref, vp_ref, vc_ref, vn_ref,
                 o_ref, lse_ref):
    tu = q_ref.shape[0]
    n_pairs = q_ref.shape[1] // 128
    i = pl.program_id(2)
    n_u = pl.num_programs(2) * tu
    row = lax.broadcasted_iota(I32, (tu, 2 * tu), 0)
    col = lax.broadcasted_iota(I32, (tu, 2 * tu), 1)
    j = col - radius - row
    key = i * tu - radius + col
    valid = (jnp.abs(j) <= radius) & (key >= 0) & (key < n_u)
    dist = (dilation * jnp.abs(j)).astype(F32)
    pen = jnp.where(valid, 0.0, NEG_BIG)
    first = lax.broadcasted_iota(I32, (1, 128), 1) < HEAD_DIM
    scale = HEAD_DIM ** -0.5
    n_heads = 2 * n_pairs
    for hp in range(n_pairs):
        lanes = slice(hp * 128, (hp + 1) * 128)
        q2 = q_ref[:, lanes]
        kw = jnp.concatenate([kp_ref[tu - radius:, lanes], kc_ref[:, lanes],
                              kn_ref[:tu - radius, lanes]], axis=0)
        vw = jnp.concatenate([vp_ref[tu - radius:, lanes], vc_ref[:, lanes],
                              vn_ref[:tu - radius, lanes]], axis=0)
        o2 = jnp.zeros((tu, 128), F32)
        lse2 = jnp.zeros((tu, 128), F32)
        for sub in range(2):
            sel = first if sub == 0 else jnp.logical_not(first)
            slope = 2.0 ** (-8.0 * (2 * hp + sub + 1) / n_heads)
            qm = jnp.where(sel, q2, jnp.zeros_like(q2))
            s = lax.dot_general(qm, kw, (((1,), (1,)), ((), ())), preferred_element_type=F32)
            s = s * scale + (pen - slope * dist)
            m = jnp.max(s, axis=1, keepdims=True)
            p = jnp.exp(s - m)
            l = jnp.sum(p, axis=1, keepdims=True)
            vm = jnp.where(sel, vw, jnp.zeros_like(vw))
            o2 = o2 + jnp.dot(p.astype(BF16), vm, preferred_element_type=F32) / l
            lse2 = jnp.where(sel, m + jnp.log(l), lse2)
        o_ref[:, lanes] = o2.astype(o_ref.dtype)
        lse_ref[:, lanes] = lse2


def _attn_b(pbd, window, tu=128):
    b, dilation, u, e = pbd.shape
    width = e // 3
    radius = window // (2 * dilation)
    assert radius < tu and u % tu == 0
    n_t = u // tu

    def spec(which, shift):
        def index(bi, r, i):
            return (bi, r, jnp.clip(i + shift, 0, n_t - 1), which)
        return pl.BlockSpec((None, None, tu, width), index)

    out_spec = pl.BlockSpec((None, None, tu, width), lambda bi, r, i: (bi, r, i, 0))
    return pl.pallas_call(
        functools.partial(_attn_b_body, dilation, radius),
        grid=(b, dilation, n_t),
        in_specs=[spec(0, 0), spec(1, -1), spec(1, 0), spec(1, 1), spec(2, -1), spec(2, 0), spec(2, 1)],
        out_specs=[out_spec, out_spec],
        out_shape=[jax.ShapeDtypeStruct((b, dilation, u, width), BF16),
                   jax.ShapeDtypeStruct((b, dilation, u, width), F32)],
        compiler_params=_params("parallel", "parallel", "parallel"),
        name=f"attn_b_d{dilation}",
    )(pbd, pbd, pbd, pbd, pbd, pbd, pbd)


def _rms(x, w):
    ms = jnp.mean(x * x, axis=-1, keepdims=True)
    return x * lax.rsqrt(ms + NORM_EPS) * w


def _to_slabs(ref, x):
    rows, width = x.shape
    c = width // 128
    for j in range(c):
        ref[pl.ds(j, rows, stride=c), :] = x[:, j * 128:(j + 1) * 128]


def _from_slabs(ref, rows, c):
    return jnp.concatenate([ref[pl.ds(j, rows, stride=c), :] for j in range(c)], axis=1)


def _token_order(ref, scr):
    d, rows, _ = ref.shape
    if d == 1:
        return ref[0].astype(F32)
    for r in range(d):
        slab = ref[r].astype(F32)
        for c in range(scr.shape[0]):
            scr[c, pl.ds(r, rows, stride=d), :] = slab[:, c * 128:(c + 1) * 128]
    return jnp.concatenate([scr[c] for c in range(scr.shape[0])], axis=1)


def _outproj_body(n_groups, *refs):
    oa_ref = refs[0]
    ob_refs = refs[1:1 + n_groups]
    lse_refs = refs[1 + n_groups:1 + 2 * n_groups]
    (x_ref, wo_a_ref, wo_b_ref, nwa_ref, nwb_ref, nm_ref, wr_ref, br_ref,
     x1_ref, hm_ref, lg_ref) = refs[1 + 2 * n_groups:12 + 2 * n_groups]
    scrs = refs[12 + 2 * n_groups:]
    obs = [_token_order(r, scrs[2 * g]) for g, r in enumerate(ob_refs)]
    lses = [_token_order(r, scrs[2 * g + 1]) for g, r in enumerate(lse_refs)]
    mx = functools.reduce(jnp.maximum, lses)
    ws = [jnp.exp(v - mx) for v in lses]
    den = functools.reduce(lambda a, c: a + c, ws)
    ob = functools.reduce(lambda a, c: a + c, [w * o for w, o in zip(ws, obs)]) / den
    na = _rms(oa_ref[...].astype(F32), nwa_ref[...]).astype(BF16)
    nb = _rms(ob, nwb_ref[...]).astype(BF16)
    x1 = (x_ref[...] + jnp.dot(na, wo_a_ref[...], preferred_element_type=F32)
          + jnp.dot(nb, wo_b_ref[...], preferred_element_type=F32))
    _to_slabs(x1_ref, x1)
    hm = _rms(x1, nm_ref[...])
    _to_slabs(hm_ref, hm)
    hm_hi = hm.astype(BF16)
    hm_lo = (hm - hm_hi.astype(F32)).astype(BF16)
    lg_ref[...] = (jnp.dot(hm_hi, wr_ref[0], preferred_element_type=F32)
                   + jnp.dot(hm_lo, wr_ref[0], preferred_element_type=F32)
                   + jnp.dot(hm_hi, wr_ref[1], preferred_element_type=F32)) + br_ref[...]


def _outproj(o_a, o_bs, lses, x2, w_out, out_norm_w, norm_moe_w, w_router, b_router, tm=256):
    n, d = x2.shape
    wa, wb = o_a.shape[1], o_bs[0].shape[3]
    n_e = w_router.shape[1]
    per_b = n // o_bs[0].shape[0] // tm
    row = lambda width: pl.BlockSpec((tm, width), lambda i: (i, 0))
    full = lambda shape: pl.BlockSpec(shape, lambda i: (0,) * len(shape))
    grouped = [pl.BlockSpec((None, o.shape[1], tm // o.shape[1], wb),
                            lambda i: (i // per_b, 0, i % per_b, 0)) for o in o_bs]
    n_groups = len(o_bs)
    c = d // 128
    slabs = pl.BlockSpec((tm * c, 128), lambda i: (i, 0))
    wr_hi = w_router.astype(BF16)
    return pl.pallas_call(
        functools.partial(_outproj_body, n_groups),
        grid=(n // tm,),
        in_specs=[row(wa)] + grouped + grouped + [
            row(d), full((wa, d)), full((wb, d)), full((1, wa)), full((1, wb)), full((1, d)),
            full((2, d, n_e)), full((1, n_e))],
        out_specs=[slabs, slabs, row(n_e)],
        out_shape=[jax.ShapeDtypeStruct((n * c, 128), F32), jax.ShapeDtypeStruct((n * c, 128), F32),
                   jax.ShapeDtypeStruct((n, n_e), F32)],
        scratch_shapes=[pltpu.VMEM((wb // 128, tm, 128), F32)] * (2 * n_groups),
        compiler_params=_params("parallel"),
        name="outproj",
    )(o_a, *o_bs, *lses, x2, w_out[:wa].astype(BF16), w_out[wa:].astype(BF16),
      out_norm_w[:wa].reshape(1, wa), out_norm_w[wa:].reshape(1, wb), norm_moe_w.reshape(1, d),
      jnp.stack([wr_hi, (w_router - wr_hi.astype(F32)).astype(BF16)]), b_router.reshape(1, n_e))


def _router_body(lg_ref, idx_ref, gate_ref, rank_ref, cnt_ref, carry_ref):
    i = pl.program_id(0)
    tm, n_e = lg_ref.shape

    @pl.when(i == 0)
    def _():
        carry_ref[...] = jnp.zeros_like(carry_ref)

    lane = lax.broadcasted_iota(I32, (tm, n_e), 1)
    work = lg_ref[...]
    vals, idxs = [], []
    for _ in range(TOP_K):
        mx = jnp.max(work, axis=1, keepdims=True)
        ix = jnp.min(jnp.where(work == mx, lane, n_e), axis=1, keepdims=True)
        vals.append(mx)
        idxs.append(ix)
        work = jnp.where(lane == ix, -jnp.inf, work)
    es = [jnp.exp(v - vals[0]) for v in vals]
    den = functools.reduce(lambda a, c: a + c, es)
    chosen = functools.reduce(lambda a, c: a + c, [(lane == ix).astype(F32) for ix in idxs])
    r = lax.broadcasted_iota(I32, (tm, tm), 0)
    c = lax.broadcasted_iota(I32, (tm, tm), 1)
    before = jnp.where(c < r, 1.0, 0.0).astype(BF16)
    prefix = jnp.dot(before, chosen.astype(BF16), preferred_element_type=F32) + carry_ref[...]
    for k in range(TOP_K):
        idx_ref[:, k:k + 1] = idxs[k]
        gate_ref[:, k:k + 1] = es[k] / den
        rank_ref[:, k:k + 1] = jnp.sum(jnp.where(lane == idxs[k], prefix, 0.0), axis=1,
                                       keepdims=True).astype(I32)
    carry_ref[...] += jnp.sum(chosen, axis=0, keepdims=True)
    cnt_ref[...] = carry_ref[...].astype(I32)


def _router(logits, tm=256):
    n, n_e = logits.shape
    row = pl.BlockSpec((tm, TOP_K), lambda i: (i, 0))
    return pl.pallas_call(
        _router_body,
        grid=(n // tm,),
        in_specs=[pl.BlockSpec((tm, n_e), lambda i: (i, 0))],
        out_specs=[row, row, row, pl.BlockSpec((1, n_e), lambda i: (0, 0))],
        out_shape=[jax.ShapeDtypeStruct((n, TOP_K), I32), jax.ShapeDtypeStruct((n, TOP_K), F32),
                   jax.ShapeDtypeStruct((n, TOP_K), I32), jax.ShapeDtypeStruct((1, n_e), I32)],
        scratch_shapes=[pltpu.VMEM((1, n_e), F32)],
        compiler_params=_params("arbitrary"),
        name="router",
    )(logits)


def _slab_copy(src_ref, row, buf_ref, slot, c, sem):
    return pltpu.make_async_copy(src_ref.at[pl.ds(pl.multiple_of(row * c, c), c)],
                                 buf_ref.at[pl.ds(pl.multiple_of(slot * c, c), c)], sem)


def _two_slot_steps(issue, consume):
    i = pl.program_id(0)

    @pl.when(i == 0)
    def _():
        issue(0, 0)

    for slot in range(2):
        @pl.when(i % 2 == slot)
        def _():
            @pl.when(i + 1 < pl.num_programs(0))
            def _():
                issue(1, 1 - slot)

            consume(slot)


def _row_gather_body(rows_per_step, c, idx_ref, nxt_ref, src_ref, dst_ref, buf_ref, sem):
    def issue(step_offset, slot):
        ref = nxt_ref if step_offset else idx_ref

        def one(r, carry):
            _slab_copy(src_ref, ref[0, 0, r], buf_ref.at[slot], r, c, sem.at[slot]).start()
            return carry

        lax.fori_loop(0, rows_per_step, one, 0, unroll=8)

    def consume(slot):
        pltpu.make_async_copy(src_ref.at[pl.ds(0, rows_per_step * c)], buf_ref.at[slot], sem.at[slot]).wait()
        dst_ref[...] = _from_slabs(buf_ref.at[slot], rows_per_step, c).astype(dst_ref.dtype)

    _two_slot_steps(issue, consume)


def _row_gather(src, idx, c, out_dtype, rows_per_step=1024):
    n_out = idx.shape[0]
    while n_out % rows_per_step:
        rows_per_step //= 2
    steps = n_out // rows_per_step
    idx3 = idx.reshape(steps, 1, rows_per_step)
    return pl.pallas_call(
        functools.partial(_row_gather_body, rows_per_step, c),
        grid=(steps,),
        in_specs=[pl.BlockSpec((1, 1, rows_per_step), lambda i: (i, 0, 0), memory_space=pltpu.SMEM),
                  pl.BlockSpec((1, 1, rows_per_step), lambda i: (jnp.minimum(i + 1, steps - 1), 0, 0),
                               memory_space=pltpu.SMEM),
                  pl.BlockSpec(memory_space=pl.ANY)],
        out_specs=pl.BlockSpec((rows_per_step, c * 128), lambda i: (i, 0)),
        out_shape=jax.ShapeDtypeStruct((n_out, c * 128), out_dtype),
        scratch_shapes=[pltpu.VMEM((2, rows_per_step * c, 128), src.dtype), pltpu.SemaphoreType.DMA((2,))],
        compiler_params=_params("arbitrary"),
        name="dispatch",
    )(idx3, idx3, src)


def _ffn_body(blk_e_ref, n_used_ref, x_ref, wg_ref, wu_ref, bg_ref, bu_ref, wd_ref, bd_ref, g_ref,
              y_ref, act_ref):
    i, f = pl.program_id(0), pl.program_id(1)
    n_f = pl.num_programs(1)
    used = i < n_used_ref[0]

    @pl.when(used)
    def _():
        x = x_ref[...]
        tf = wg_ref.shape[2]
        gate = jnp.dot(x, wg_ref[0], preferred_element_type=F32) + bg_ref[0]
        up = jnp.dot(x, wu_ref[0], preferred_element_type=F32) + bu_ref[0]
        gate = jnp.minimum(gate, SWIGLU_LIMIT)
        up = jnp.clip(up, -SWIGLU_LIMIT, SWIGLU_LIMIT)
        act = (up + 1.0) * gate * jax.nn.sigmoid(SWIGLU_ALPHA * gate)
        act_ref[:, pl.ds(pl.multiple_of(f * tf, tf), tf)] = act.astype(BF16)

        @pl.when(f == n_f - 1)
        def _():
            y = jnp.dot(act_ref[...], wd_ref[0], preferred_element_type=F32)
            _to_slabs(y_ref, (y + bd_ref[0]) * g_ref[...])

    @pl.when(jnp.logical_not(used) & (f == n_f - 1))
    def _():
        y_ref[...] = jnp.zeros_like(y_ref)


def _expert_ffn(xs, blk_e, n_used, w_gu, b_gu, w_down, b_down, gate_buf, tf=512):
    cap, d = xs.shape
    n_e, _, two_f = w_gu.shape
    ff = two_f // 2
    n_f = ff // tf
    n_blocks = cap // MOE_BLOCK

    def fidx(i, f, nu):
        return jnp.where(i < nu[0], f, n_f - 1)

    grid_spec = pltpu.PrefetchScalarGridSpec(
        num_scalar_prefetch=2,
        grid=(n_blocks, n_f),
        in_specs=[
            pl.BlockSpec((MOE_BLOCK, d), lambda i, f, be, nu: (jnp.minimum(i, nu[0] - 1), 0)),
            pl.BlockSpec((1, d, tf), lambda i, f, be, nu: (be[i], 0, fidx(i, f, nu))),
            pl.BlockSpec((1, d, tf), lambda i, f, be, nu: (be[i], 0, n_f + fidx(i, f, nu))),
            pl.BlockSpec((1, 1, tf), lambda i, f, be, nu: (be[i], 0, fidx(i, f, nu))),
            pl.BlockSpec((1, 1, tf), lambda i, f, be, nu: (be[i], 0, n_f + fidx(i, f, nu))),
            pl.BlockSpec((1, ff, d), lambda i, f, be, nu: (be[i], 0, 0)),
            pl.BlockSpec((1, 1, d), lambda i, f, be, nu: (be[i], 0, 0)),
            pl.BlockSpec((MOE_BLOCK, 1), lambda i, f, be, nu: (i, 0)),
        ],
        out_specs=pl.BlockSpec((MOE_BLOCK * (d // 128), 128), lambda i, f, be, nu: (i, 0)),
        scratch_shapes=[pltpu.VMEM((MOE_BLOCK, ff), BF16)],
    )
    return pl.pallas_call(
        _ffn_body,
        grid_spec=grid_spec,
        out_shape=jax.ShapeDtypeStruct((cap * (d // 128), 128), F32),
        compiler_params=_params("arbitrary", "arbitrary"),
        name="expert_ffn",
    )(blk_e, n_used, xs, w_gu, w_gu, b_gu.reshape(n_e, 1, two_f), b_gu.reshape(n_e, 1, two_f),
      w_down, b_down.reshape(n_e, 1, d), gate_buf.reshape(cap, 1))


def _combine_body(tm, c, dest_ref, nxt_ref, y_ref, x1_ref, w_ref, o_ref, buf_ref, sum_ref, sem):
    def issue(step_offset, slot):
        ref = nxt_ref if step_offset else dest_ref

        def one(t, carry):
            for k in range(TOP_K):
                _slab_copy(y_ref, ref[0, 0, t * TOP_K + k], buf_ref.at[slot, k], t, c, sem.at[slot]).start()
            return carry

        lax.fori_loop(0, tm, one, 0, unroll=2)

    def consume(slot):
        for k in range(TOP_K):
            pltpu.make_async_copy(y_ref.at[pl.ds(0, tm * c)], buf_ref.at[slot, k], sem.at[slot]).wait()
        x = x1_ref[...]
        for k in range(TOP_K):
            x = x + buf_ref[slot, k]
        x = x.reshape(tm, c, 128)
        ms = jnp.sum(jnp.sum(x * x, axis=2, keepdims=True), axis=1, keepdims=True) / (c * 128)
        sum_ref[...] = (x * lax.rsqrt(ms + NORM_EPS)).reshape(tm * c, 128)
        o_ref[...] = _from_slabs(sum_ref, tm, c) * w_ref[...]

    _two_slot_steps(issue, consume)


def _combine(y, dest, x1, w, c, tm=256):
    n = x1.shape[0] // c
    steps = n // tm
    dest3 = dest.reshape(steps, 1, tm * TOP_K)
    return pl.pallas_call(
        functools.partial(_combine_body, tm, c),
        grid=(steps,),
        in_specs=[pl.BlockSpec((1, 1, tm * TOP_K), lambda i: (i, 0, 0), memory_space=pltpu.SMEM),
                  pl.BlockSpec((1, 1, tm * TOP_K), lambda i: (jnp.minimum(i + 1, steps - 1), 0, 0),
                               memory_space=pltpu.SMEM),
                  pl.BlockSpec(memory_space=pl.ANY),
                  pl.BlockSpec((tm * c, 128), lambda i: (i, 0)),
                  pl.BlockSpec((1, c * 128), lambda i: (0, 0))],
        out_specs=pl.BlockSpec((tm, c * 128), lambda i: (i, 0)),
        out_shape=jax.ShapeDtypeStruct((n, c * 128), F32),
        scratch_shapes=[pltpu.VMEM((2, TOP_K, tm * c, 128), F32), pltpu.VMEM((tm * c, 128), F32),
                        pltpu.SemaphoreType.DMA((2,))],
        compiler_params=_params("arbitrary"),
        name="combine",
    )(dest3, dest3, y, x1, w.reshape(1, c * 128))


def _layer(x, norm_mix_w, w_in, q_norm_w, k_norm_w, out_norm_w, w_out, norm_moe_w,
           w_router, b_router, w_gate_up, b_gate_up, w_down, b_down):
    b, s, d = x.shape
    n = b * s
    x2 = x.reshape(n, d)
    b_w = d // 4
    a_q_w = d - b_w
    a_kv_w = a_q_w // GQA_RATIO

    h = _prenorm(x2, norm_mix_w)
    qT, k, vT = _aproj(h.reshape(b, s, d), w_in[:, :a_q_w], w_in[:, a_q_w:a_q_w + a_kv_w],
                       w_in[:, a_q_w + a_kv_w:a_q_w + 2 * a_kv_w], q_norm_w, k_norm_w)
    score_bound = (1.02 * HEAD_DIM ** 0.5 * LOG2E) * jnp.max(jnp.abs(q_norm_w)) * jnp.max(jnp.abs(k_norm_w))
    o_a = _attn_a(qT, k, vT, score_bound).reshape(n, a_q_w)

    o_bs, lses = [], []
    base = a_q_w + 2 * a_kv_w
    for g, (window, dilation) in enumerate(B_CONFIGS):
        w_g = w_in[:, base + 3 * g * b_w:base + 3 * (g + 1) * b_w].astype(BF16)
        o_g, lse_g = _attn_b(_bproj(h, w_g, b, dilation), window)
        o_bs.append(o_g)
        lses.append(lse_g)

    x1, hm, logits = _outproj(o_a, o_bs, lses, x2, w_out, out_norm_w, norm_moe_w, w_router, b_router)

    top_idx, gates, rank, counts = _router(logits)
    counts = counts.reshape(N_EXPERTS)
    padded = (counts + MOE_BLOCK - 1) // MOE_BLOCK * MOE_BLOCK
    pend = jnp.cumsum(padded)
    pstart = pend - padded
    dest = pstart[top_idx] + rank
    n_blocks = -(-(n * TOP_K) // MOE_BLOCK) + N_EXPERTS
    cap = n_blocks * MOE_BLOCK
    n_assign = n * TOP_K
    assign = jnp.full((cap,), n_assign, I32).at[dest.reshape(-1)].set(jnp.arange(n_assign, dtype=I32))
    tok_buf = jnp.minimum(assign // TOP_K, n - 1)
    gate_buf = jnp.concatenate([gates.reshape(-1), jnp.zeros((1,), F32)])[assign]
    blk_start = jnp.arange(n_blocks, dtype=I32) * MOE_BLOCK
    blk_e = jnp.minimum(jnp.sum((pend[None, :] <= blk_start[:, None]).astype(I32), axis=1), N_EXPERTS - 1)
    n_used = (pend[-1] // MOE_BLOCK).astype(I32).reshape(1)

    xs = _row_gather(hm, tok_buf, d // 128, BF16)
    y = _expert_ffn(xs, blk_e, n_used, w_gate_up.astype(BF16), b_gate_up, w_down.astype(BF16), b_down, gate_buf)
    return y, dest, x1


def kernel(x, norm_mix_w, w_in, q_norm_w, k_norm_w, out_norm_w, w_out, norm_moe_w, w_router, b_router,
           w_gate_up, b_gate_up, w_down, b_down, final_norm_w):
    b, s, d = x.shape
    assert w_in.shape[0] == 1, "the MoE combine is fused with the final norm: single-layer stacks only"
    y, dest, x1 = _layer(x, norm_mix_w[0], w_in[0], q_norm_w[0], k_norm_w[0], out_norm_w[0], w_out[0],
                         norm_moe_w[0], w_router[0], b_router[0], w_gate_up[0], b_gate_up[0],
                         w_down[0], b_down[0])
    return _combine(y, dest, x1, final_norm_w, d // 128).reshape(b, s, d)
```

```python
import functools

import jax
import jax.numpy as jnp
from jax import lax
from jax.experimental import pallas as pl
from jax.experimental.pallas import tpu as pltpu

F32 = jnp.float32
BF16 = jnp.bfloat16
I32 = jnp.int32

HEAD_DIM = 64
GQA_RATIO = 4
B_CONFIGS = ((128, 1), (512, 4), (2048, 16))
GRID_W = 64
ROPE_THETA = 10000.0
N_EXPERTS = 32
TOP_K = 4
MOE_BLOCK = 512
SWIGLU_LIMIT = 7.0
SWIGLU_ALPHA = 1.702
NORM_EPS = 1e-5
QK_EPS = 1e-6
LOG2E = 1.4426950408889634
NEG_BIG = -1e30
V_PAD_ROWS = 16
MAX_SAFE_LOG2_SCORE = 40.0
V7X_VMEM_LIMIT = 56 * 1024 * 1024


def _params(*sem):
    return pltpu.CompilerParams(dimension_semantics=sem, vmem_limit_bytes=V7X_VMEM_LIMIT)


def _prenorm_body(x_ref, w_ref, o_ref):
    x = x_ref[...]
    ms = jnp.mean(x * x, axis=-1, keepdims=True)
    o_ref[...] = (x * lax.rsqrt(ms + NORM_EPS) * w_ref[...]).astype(o_ref.dtype)


def _prenorm(x2, w, tm=512):
    n, d = x2.shape
    return pl.pallas_call(
        _prenorm_body,
        grid=(n // tm,),
        in_specs=[pl.BlockSpec((tm, d), lambda i: (i, 0)), pl.BlockSpec((1, d), lambda i: (0, 0))],
        out_specs=pl.BlockSpec((tm, d), lambda i: (i, 0)),
        out_shape=jax.ShapeDtypeStruct((n, d), BF16),
        compiler_params=_params("parallel"),
        name="prenorm",
    )(x2, w.reshape(1, d))


def _aproj_body(n_q, n_kv, h_ref, wqv_ref, wk_ref, qnw_ref, knw_ref, cos_ref, sin_ref,
                kc_ref, ksm_ref, ksp_ref, bd_ref, qT_ref, k_ref, vT_ref, pt_ref):
    h = h_ref[0]
    pt_ref[...] = lax.dot_general(wqv_ref[...], h, (((1,), (1,)), ((), ())),
                                  preferred_element_type=F32)
    q4 = HEAD_DIM // 4
    cr, cc = cos_ref[0:q4, :], cos_ref[q4:2 * q4, :]
    sr, sc = sin_ref[0:q4, :], sin_ref[q4:2 * q4, :]
    qnw = qnw_ref[...]

    def q_head(hh, carry):
        y = pt_ref[pl.ds(pl.multiple_of(hh * HEAD_DIM, HEAD_DIM), HEAD_DIM), :]
        ms = jnp.mean(y * y, axis=0, keepdims=True)
        y = y * lax.rsqrt(ms + QK_EPS) * qnw
        a1, a2, b1, b2 = y[0:q4], y[q4:2 * q4], y[2 * q4:3 * q4], y[3 * q4:]
        out = jnp.concatenate([a1 * cr - a2 * sr, a2 * cr + a1 * sr,
                               b1 * cc - b2 * sc, b2 * cc + b1 * sc], axis=0)
        qT_ref[0, hh] = (out * (HEAD_DIM ** -0.5 * LOG2E)).astype(BF16)
        return carry

    lax.fori_loop(0, n_q, q_head, 0)
    ones_row = (lax.broadcasted_iota(I32, (V_PAD_ROWS, pt_ref.shape[1]), 0) == 0).astype(BF16)
    for g in range(n_kv):
        lo = (n_q + g) * HEAD_DIM
        vT_ref[0, g, :HEAD_DIM] = pt_ref[lo:lo + HEAD_DIM, :].astype(BF16)
        vT_ref[0, g, HEAD_DIM:] = ones_row

    kn = jnp.dot(h, wk_ref[...], preferred_element_type=F32)
    ms = jnp.dot(kn * kn, bd_ref[...], precision=lax.Precision.HIGHEST,
                 preferred_element_type=F32)
    kn = kn * lax.rsqrt(ms + QK_EPS) * knw_ref[...]
    for c in range(n_kv // 2):
        y = kn[:, c * 128:(c + 1) * 128]
        out = (y * kc_ref[...] + pltpu.roll(y, 128 - q4, 1) * ksm_ref[...]
               + pltpu.roll(y, q4, 1) * ksp_ref[...])
        k_ref[0, 2 * c] = out[:, :HEAD_DIM].astype(BF16)
        k_ref[0, 2 * c + 1] = out[:, HEAD_DIM:].astype(BF16)


def _rope_tables(seq_len):
    rows = seq_len // GRID_W
    r, c = jnp.meshgrid(jnp.arange(rows), jnp.arange(GRID_W), indexing="ij")
    axis_dim = HEAD_DIM // 2
    inv = ROPE_THETA ** (-jnp.arange(0, axis_dim, 2, dtype=F32) / axis_dim)
    ang_r = r.reshape(-1).astype(F32)[:, None] * inv[None, :]
    ang_c = c.reshape(-1).astype(F32)[:, None] * inv[None, :]
    ang = jnp.concatenate([ang_r, ang_c], axis=-1)
    return jnp.cos(ang), jnp.sin(ang)


def _aproj(h3, w_q, w_k, w_v, q_norm_w, k_norm_w, ts=512):
    b, s, d = h3.shape
    n_q, n_kv = w_q.shape[1] // HEAD_DIM, w_k.shape[1] // HEAD_DIM
    q4 = HEAD_DIM // 4
    cos, sin = _rope_tables(s)
    zeros = jnp.zeros_like(sin[:, :q4])
    c64 = jnp.concatenate([cos[:, :q4], cos[:, :q4], cos[:, q4:], cos[:, q4:]], axis=1)
    sm64 = jnp.concatenate([-sin[:, :q4], zeros, -sin[:, q4:], zeros], axis=1)
    sp64 = jnp.concatenate([zeros, sin[:, :q4], zeros, sin[:, q4:]], axis=1)
    kc, ksm, ksp = (jnp.concatenate([t, t], axis=1) for t in (c64, sm64, sp64))
    head_of = jnp.arange(n_kv * HEAD_DIM) // HEAD_DIM
    bd = (head_of[:, None] == head_of[None, :]).astype(F32) / HEAD_DIM
    wqv_t = jnp.concatenate([w_q, w_v], axis=1).T.astype(BF16)
    rows = wqv_t.shape[0]
    full = lambda shape: pl.BlockSpec(shape, lambda bi, i: (0,) * len(shape))
    return pl.pallas_call(
        functools.partial(_aproj_body, n_q, n_kv),
        grid=(b, s // ts),
        in_specs=[
            pl.BlockSpec((1, ts, d), lambda bi, i: (bi, i, 0)),
            full((rows, d)), full((d, n_kv * HEAD_DIM)),
            full((HEAD_DIM, 1)), full((1, n_kv * HEAD_DIM)),
            pl.BlockSpec((2 * q4, ts), lambda bi, i: (0, i)),
            pl.BlockSpec((2 * q4, ts), lambda bi, i: (0, i)),
            pl.BlockSpec((ts, 128), lambda bi, i: (i, 0)),
            pl.BlockSpec((ts, 128), lambda bi, i: (i, 0)),
            pl.BlockSpec((ts, 128), lambda bi, i: (i, 0)),
            full((n_kv * HEAD_DIM, n_kv * HEAD_DIM)),
        ],
        out_specs=[
            pl.BlockSpec((1, n_q, HEAD_DIM, ts), lambda bi, i: (bi, 0, 0, i)),
            pl.BlockSpec((1, n_kv, ts, HEAD_DIM), lambda bi, i: (bi, 0, i, 0)),
            pl.BlockSpec((1, n_kv, HEAD_DIM + V_PAD_ROWS, ts), lambda bi, i: (bi, 0, 0, i)),
        ],
        out_shape=[
            jax.ShapeDtypeStruct((b, n_q, HEAD_DIM, s), BF16),
            jax.ShapeDtypeStruct((b, n_kv, s, HEAD_DIM), BF16),
            jax.ShapeDtypeStruct((b, n_kv, HEAD_DIM + V_PAD_ROWS, s), BF16),
        ],
        scratch_shapes=[pltpu.VMEM((rows, ts), F32)],
        compiler_params=_params("parallel", "parallel"),
        name="aproj",
    )(h3, wqv_t, w_k.astype(BF16), q_norm_w.reshape(HEAD_DIM, 1),
      jnp.tile(k_norm_w, n_kv).reshape(1, n_kv * HEAD_DIM), cos.T, sin.T, kc, ksm, ksp, bd)


def _attn_a_body(tk, qT_ref, k_ref, vT_ref, o_ref, q_scr, acc_scr, sa_scr, sb_scr, p_scr):
    s_len = k_ref.shape[2]
    tq = qT_ref.shape[3]
    w = GQA_RATIO * tq
    n_steps = s_len // tk
    rc = 32
    for hh in range(GQA_RATIO):
        q_scr[:, hh * tq:(hh + 1) * tq] = qT_ref[0, hh]
    acc_scr[...] = jnp.zeros(acc_scr.shape, F32)

    def scores(j, s_scr):
        off = pl.multiple_of(j * tk, tk)
        s_scr[...] = jnp.dot(k_ref[0, 0, pl.ds(off, tk), :], q_scr[...], preferred_element_type=F32)

    def half_step(j, m, s_scr, nxt_scr):
        scores(jnp.minimum(j + 1, n_steps - 1), nxt_scr)
        off = pl.multiple_of(j * tk, tk)
        v = vT_ref[0, 0, :, pl.ds(off, tk)]
        m8 = jnp.full((8, w), NEG_BIG, F32)
        for c in range(tk // rc):
            sc = s_scr[c * rc:(c + 1) * rc, :]
            m8 = jnp.maximum(m8, jnp.max(sc.reshape(rc // 8, 8, w), axis=0))
        m_new = jnp.maximum(m, jnp.max(m8, axis=0, keepdims=True))
        for c in range(tk // rc):
            sc = s_scr[c * rc:(c + 1) * rc, :]
            p_scr[c * rc:(c + 1) * rc, :] = jnp.exp2(sc - m_new).astype(BF16)
        alpha = jnp.exp2(m - m_new)
        acc_scr[...] = alpha * acc_scr[...] + jnp.dot(v, p_scr[...], preferred_element_type=F32)
        return m_new

    def kv_step(jj, m):
        m = half_step(2 * jj, m, sa_scr, sb_scr)
        return half_step(2 * jj + 1, m, sb_scr, sa_scr)

    scores(0, sa_scr)
    lax.fori_loop(0, n_steps // 2, kv_step, jnp.full((1, w), NEG_BIG, F32))
    o = acc_scr[:HEAD_DIM, :] / acc_scr[HEAD_DIM:HEAD_DIM + 1, :]
    for hh in range(GQA_RATIO):
        o_ref[0, :, hh * HEAD_DIM:(hh + 1) * HEAD_DIM] = o[:, hh * tq:(hh + 1) * tq].T.astype(o_ref.dtype)


def _attn_a_bounded_body(tk, qT_ref, k_ref, vT_ref, o_ref, q_scr, pa_scr, pb_scr, acc_scr):
    s_len = k_ref.shape[2]
    tq = qT_ref.shape[3]
    n_steps = s_len // tk
    for hh in range(GQA_RATIO):
        q_scr[:, hh * tq:(hh + 1) * tq] = qT_ref[0, hh]
    acc_scr[...] = jnp.zeros(acc_scr.shape, F32)

    def chunk(j):
        return pl.ds(j * tk if isinstance(j, int) else pl.multiple_of(j * tk, tk), tk)

    def probs(j, p_scr):
        s = jnp.dot(k_ref[0, 0, chunk(j), :], q_scr[...], preferred_element_type=F32)
        p_scr[...] = jnp.exp2(s).astype(BF16)

    def half_step(j, p_scr, nxt_scr):
        if nxt_scr is not None:
            probs(j + 1, nxt_scr)
        acc_scr[...] += jnp.dot(vT_ref[0, 0, :, chunk(j)], p_scr[...], preferred_element_type=F32)

    def kv_step(jj, carry):
        half_step(2 * jj, pa_scr, pb_scr)
        half_step(2 * jj + 1, pb_scr, pa_scr)
        return carry

    probs(0, pa_scr)
    lax.fori_loop(0, n_steps // 2 - 1, kv_step, 0)
    half_step(n_steps - 2, pa_scr, pb_scr)
    half_step(n_steps - 1, pb_scr, None)
    o = acc_scr[:HEAD_DIM, :] / acc_scr[HEAD_DIM:HEAD_DIM + 1, :]
    for hh in range(GQA_RATIO):
        o_ref[0, :, hh * HEAD_DIM:(hh + 1) * HEAD_DIM] = o[:, hh * tq:(hh + 1) * tq].T.astype(o_ref.dtype)


def _attn_a_call(body, tq, scratch, name, qT, k, vT):
    b, n_q, _, s = qT.shape
    n_kv, vr = k.shape[1], vT.shape[2]
    return pl.pallas_call(
        body,
        grid=(b, n_kv, s // tq),
        in_specs=[
            pl.BlockSpec((1, GQA_RATIO, HEAD_DIM, tq), lambda bi, g, i: (bi, g, 0, i)),
            pl.BlockSpec((1, 1, s, HEAD_DIM), lambda bi, g, i: (bi, g, 0, 0)),
            pl.BlockSpec((1, 1, vr, s), lambda bi, g, i: (bi, g, 0, 0)),
        ],
        out_specs=pl.BlockSpec((1, tq, GQA_RATIO * HEAD_DIM), lambda bi, g, i: (bi, i, g)),
        out_shape=jax.ShapeDtypeStruct((b, s, n_q * HEAD_DIM), BF16),
        scratch_shapes=scratch,
        compiler_params=_params("parallel", "parallel", "parallel"),
        name=name,
    )(qT, k, vT)


def _attn_a(qT, k, vT, score_bound, tq=256, tk=256, tq_bounded=512, tk_bounded=1024):
    s = qT.shape[3]
    vr = vT.shape[2]
    tq_bounded, tk_bounded = min(tq_bounded, s), min(tk_bounded, s // 2)
    assert s % (2 * tk) == 0 and s % tq == 0 and s % (2 * tk_bounded) == 0 and s % tq_bounded == 0
    w, wb = GQA_RATIO * tq, GQA_RATIO * tq_bounded
    general = functools.partial(
        _attn_a_call, functools.partial(_attn_a_body, tk), tq,
        [pltpu.VMEM((HEAD_DIM, w), BF16), pltpu.VMEM((vr, w), F32),
         pltpu.VMEM((tk, w), F32), pltpu.VMEM((tk, w), F32), pltpu.VMEM((tk, w), BF16)], "attn_a")
    bounded = functools.partial(
        _attn_a_call, functools.partial(_attn_a_bounded_body, tk_bounded), tq_bounded,
        [pltpu.VMEM((HEAD_DIM, wb), BF16), pltpu.VMEM((tk_bounded, wb), BF16),
         pltpu.VMEM((tk_bounded, wb), BF16), pltpu.VMEM((vr, wb), F32)], "attn_a_bounded")
    return lax.cond(score_bound <= MAX_SAFE_LOG2_SCORE, bounded, general, qT, k, vT)


def _bproj_body(dilation, h_ref, w_ref, o_ref, scr):
    res = jnp.dot(h_ref[...], w_ref[...], preferred_element_type=F32)
    if dilation == 1:
        o_ref[0] = res.astype(o_ref.dtype)
    else:
        rows = scr.shape[1] // dilation
        for c in range(scr.shape[0]):
            scr[c] = res[:, c * 128:(c + 1) * 128]
        for r in range(dilation):
            for c in range(scr.shape[0]):
                o_ref[r, :, c * 128:(c + 1) * 128] = scr[c, pl.ds(r, rows, stride=dilation), :].astype(o_ref.dtype)


def _bproj(h, w, batch, dilation, tm=512):
    n, d = h.shape
    e = w.shape[1]
    s = n // batch
    per_b = s // tm
    assert tm % (16 * dilation) == 0
    return pl.pallas_call(
        functools.partial(_bproj_body, dilation),
        grid=(n // tm,),
        in_specs=[pl.BlockSpec((tm, d), lambda i: (i, 0)), pl.BlockSpec((d, e), lambda i: (0, 0))],
        out_specs=pl.BlockSpec((None, dilation, tm // dilation, e),
                               lambda i: (i // per_b, 0, i % per_b, 0)),
        out_shape=jax.ShapeDtypeStruct((batch, dilation, s // dilation, e), BF16),
        scratch_shapes=[pltpu.VMEM((e // 128, tm, 128), F32)],
        compiler_params=_params("parallel"),
        name=f"bproj_d{dilation}",
    )(h, w)


def _attn_b_body(dilation, radius, q_ref, kp_ref, kc_ref, kn_ref, vp_ref, vc_ref, vn_ref,
                 o_ref, lse_ref):
    tu = q_ref.shape[0]
    n_pairs = q_ref.shape[1] // 128
    i = pl.program_id(2)
    n_u = pl.num_programs(2) * tu
    row = lax.broadcasted_iota(I32, (tu, 2 * tu), 0)
    col = lax.broadcasted_iota(I32, (tu, 2 * tu), 1)
    j = col - radius - row
    key = i * tu - radius + col
    valid = (jnp.abs(j) <= radius) & (key >= 0) & (key < n_u)
    dist = (dilation * jnp.abs(j)).astype(F32)
    pen = jnp.where(valid, 0.0, NEG_BIG)
    first = lax.broadcasted_iota(I32, (1, 128), 1) < HEAD_DIM
    scale = HEAD_DIM ** -0.5
    n_heads = 2 * n_pairs
    for hp in range(n_pairs):
        lanes = slice(hp * 128, (hp + 1) * 128)
        q2 = q_ref[:, lanes]
        kw = jnp.concatenate([kp_ref[tu - radius:, lanes], kc_ref[:, lanes],
                              kn_ref[:tu - radius, lanes]], axis=0)
        vw = jnp.concatenate([vp_ref[tu - radius:, lanes], vc_ref[:, lanes],
                              vn_ref[:tu - radius, lanes]], axis=0)
        o2 = jnp.zeros((tu, 128), F32)
        lse2 = jnp.zeros((tu, 128), F32)
        for sub in range(2):
            sel = first if sub == 0 else jnp.logical_not(first)
            slope = 2.0 ** (-8.0 * (2 * hp + sub + 1) / n_heads)
            qm = jnp.where(sel, q2, jnp.zeros_like(q2))
            s = lax.dot_general(qm, kw, (((1,), (1,)), ((), ())), preferred_element_type=F32)
            s = s * scale + (pen - slope * dist)
            m = jnp.max(s, axis=1, keepdims=True)
            p = jnp.exp(s - m)
            l = jnp.sum(p, axis=1, keepdims=True)
            vm = jnp.where(sel, vw, jnp.zeros_like(vw))
            o2 = o2 + jnp.dot(p.astype(BF16), vm, preferred_element_type=F32) / l
            lse2 = jnp.where(sel, m + jnp.log(l), lse2)
        o_ref[:, lanes] = o2.astype(o_ref.dtype)
        lse_ref[:, lanes] = lse2


def _attn_b(pbd, window, tu=128):
    b, dilation, u, e = pbd.shape
    width = e // 3
    radius = window // (2 * dilation)
    assert radius < tu and u % tu == 0
    n_t = u // tu

    def spec(which, shift):
        def index(bi, r, i):
            return (bi, r, jnp.clip(i + shift, 0, n_t - 1), which)
        return pl.BlockSpec((None, None, tu, width), index)

    out_spec = pl.BlockSpec((None, None, tu, width), lambda bi, r, i: (bi, r, i, 0))
    return pl.pallas_call(
        functools.partial(_attn_b_body, dilation, radius),
        grid=(b, dilation, n_t),
        in_specs=[spec(0, 0), spec(1, -1), spec(1, 0), spec(1, 1), spec(2, -1), spec(2, 0), spec(2, 1)],
        out_specs=[out_spec, out_spec],
        out_shape=[jax.ShapeDtypeStruct((b, dilation, u, width), BF16),
                   jax.ShapeDtypeStruct((b, dilation, u, width), F32)],
        compiler_params=_params("parallel", "parallel", "parallel"),
        name=f"attn_b_d{dilation}",
    )(pbd, pbd, pbd, pbd, pbd, pbd, pbd)


def _rms(x, w):
    ms = jnp.mean(x * x, axis=-1, keepdims=True)
    return x * lax.rsqrt(ms + NORM_EPS) * w


def _to_slabs(ref, x):
    rows, width = x.shape
    c = width // 128
    for j in range(c):
        ref[pl.ds(j, rows, stride=c), :] = x[:, j * 128:(j + 1) * 128]


def _from_slabs(ref, rows, c):
    return jnp.concatenate([ref[pl.ds(j, rows, stride=c), :] for j in range(c)], axis=1)


def _token_order(ref, scr):
    d, rows, _ = ref.shape
    if d == 1:
        return ref[0].astype(F32)
    for r in range(d):
        slab = ref[r].astype(F32)
        for c in range(scr.shape[0]):
            scr[c, pl.ds(r, rows, stride=d), :] = slab[:, c * 128:(c + 1) * 128]
    return jnp.concatenate([scr[c] for c in range(scr.shape[0])], axis=1)


def _outproj_body(n_groups, *refs):
    oa_ref = refs[0]
    ob_refs = refs[1:1 + n_groups]
    lse_refs = refs[1 + n_groups:1 + 2 * n_groups]
    (x_ref, wo_a_ref, wo_b_ref, nwa_ref, nwb_ref, nm_ref, wr_ref, br_ref,
     x1_ref, hm_ref, lg_ref) = refs[1 + 2 * n_groups:12 + 2 * n_groups]
    scrs = refs[12 + 2 * n_groups:]
    obs = [_token_order(r, scrs[2 * g]) for g, r in enumerate(ob_refs)]
    lses = [_token_order(r, scrs[2 * g + 1]) for g, r in enumerate(lse_refs)]
    mx = functools.reduce(jnp.maximum, lses)
    ws = [jnp.exp(v - mx) for v in lses]
    den = functools.reduce(lambda a, c: a + c, ws)
    ob = functools.reduce(lambda a, c: a + c, [w * o for w, o in zip(ws, obs)]) / den
    na = _rms(oa_ref[...].astype(F32), nwa_ref[...]).astype(BF16)
    nb = _rms(ob, nwb_ref[...]).astype(BF16)
    x1 = (x_ref[...] + jnp.dot(na, wo_a_ref[...], preferred_element_type=F32)
          + jnp.dot(nb, wo_b_ref[...], preferred_element_type=F32))
    _to_slabs(x1_ref, x1)
    hm = _rms(x1, nm_ref[...])
    hm_hi = hm.astype(BF16)
    half = hm.shape[1] // 2
    bits = lax.bitcast_convert_type(hm_hi.astype(F32), jnp.uint32)
    _to_slabs(hm_ref, bits[:, :half] | (bits[:, half:] >> 16))
    hm_lo = (hm - hm_hi.astype(F32)).astype(BF16)
    lg_ref[...] = (jnp.dot(hm_hi, wr_ref[0], preferred_element_type=F32)
                   + jnp.dot(hm_lo, wr_ref[0], preferred_element_type=F32)
                   + jnp.dot(hm_hi, wr_ref[1], preferred_element_type=F32)) + br_ref[...]


def _outproj(o_a, o_bs, lses, x2, w_out, out_norm_w, norm_moe_w, w_router, b_router, tm=256):
    n, d = x2.shape
    wa, wb = o_a.shape[1], o_bs[0].shape[3]
    n_e = w_router.shape[1]
    per_b = n // o_bs[0].shape[0] // tm
    row = lambda width: pl.BlockSpec((tm, width), lambda i: (i, 0))
    full = lambda shape: pl.BlockSpec(shape, lambda i: (0,) * len(shape))
    grouped = [pl.BlockSpec((None, o.shape[1], tm // o.shape[1], wb),
                            lambda i: (i // per_b, 0, i % per_b, 0)) for o in o_bs]
    n_groups = len(o_bs)
    c = d // 128
    slabs = pl.BlockSpec((tm * c, 128), lambda i: (i, 0))
    wr_hi = w_router.astype(BF16)
    return pl.pallas_call(
        functools.partial(_outproj_body, n_groups),
        grid=(n // tm,),
        in_specs=[row(wa)] + grouped + grouped + [
            row(d), full((wa, d)), full((wb, d)), full((1, wa)), full((1, wb)), full((1, d)),
            full((2, d, n_e)), full((1, n_e))],
        out_specs=[slabs, pl.BlockSpec((tm * c // 2, 128), lambda i: (i, 0)), row(n_e)],
        out_shape=[jax.ShapeDtypeStruct((n * c, 128), F32), jax.ShapeDtypeStruct((n * c // 2, 128), jnp.uint32),
                   jax.ShapeDtypeStruct((n, n_e), F32)],
        scratch_shapes=[pltpu.VMEM((wb // 128, tm, 128), F32)] * (2 * n_groups),
        compiler_params=_params("parallel"),
        name="outproj",
    )(o_a, *o_bs, *lses, x2, w_out[:wa].astype(BF16), w_out[wa:].astype(BF16),
      out_norm_w[:wa].reshape(1, wa), out_norm_w[wa:].reshape(1, wb), norm_moe_w.reshape(1, d),
      jnp.stack([wr_hi, (w_router - wr_hi.astype(F32)).astype(BF16)]), b_router.reshape(1, n_e))


def _router_body(lg_ref, idx_ref, gate_ref, rank_ref, cnt_ref, carry_ref):
    i = pl.program_id(0)
    tm, n_e = lg_ref.shape

    @pl.when(i == 0)
    def _():
        carry_ref[...] = jnp.zeros_like(carry_ref)

    lane = lax.broadcasted_iota(I32, (tm, n_e), 1)
    work = lg_ref[...]
    vals, idxs = [], []
    for _ in range(TOP_K):
        mx = jnp.max(work, axis=1, keepdims=True)
        ix = jnp.min(jnp.where(work == mx, lane, n_e), axis=1, keepdims=True)
        vals.append(mx)
        idxs.append(ix)
        work = jnp.where(lane == ix, -jnp.inf, work)
    es = [jnp.exp(v - vals[0]) for v in vals]
    den = functools.reduce(lambda a, c: a + c, es)
    chosen = functools.reduce(lambda a, c: a + c, [(lane == ix).astype(F32) for ix in idxs])
    r = lax.broadcasted_iota(I32, (tm, tm), 0)
    c = lax.broadcasted_iota(I32, (tm, tm), 1)
    before = jnp.where(c < r, 1.0, 0.0).astype(BF16)
    prefix = jnp.dot(before, chosen.astype(BF16), preferred_element_type=F32) + carry_ref[...]
    for k in range(TOP_K):
        idx_ref[:, k:k + 1] = idxs[k]
        gate_ref[:, k:k + 1] = es[k] / den
        rank_ref[:, k:k + 1] = jnp.sum(jnp.where(lane == idxs[k], prefix, 0.0), axis=1,
                                       keepdims=True).astype(I32)
    carry_ref[...] += jnp.sum(chosen, axis=0, keepdims=True)
    cnt_ref[...] = carry_ref[...].astype(I32)


def _router(logits, tm=256):
    n, n_e = logits.shape
    row = pl.BlockSpec((tm, TOP_K), lambda i: (i, 0))
    return pl.pallas_call(
        _router_body,
        grid=(n // tm,),
        in_specs=[pl.BlockSpec((tm, n_e), lambda i: (i, 0))],
        out_specs=[row, row, row, pl.BlockSpec((1, n_e), lambda i: (0, 0))],
        out_shape=[jax.ShapeDtypeStruct((n, TOP_K), I32), jax.ShapeDtypeStruct((n, TOP_K), F32),
                   jax.ShapeDtypeStruct((n, TOP_K), I32), jax.ShapeDtypeStruct((1, n_e), I32)],
        scratch_shapes=[pltpu.VMEM((1, n_e), F32)],
        compiler_params=_params("arbitrary"),
        name="router",
    )(logits)


def _slab_copy(src_ref, row, buf_ref, slot, c, sem):
    return pltpu.make_async_copy(src_ref.at[pl.ds(pl.multiple_of(row * c, c), c)],
                                 buf_ref.at[pl.ds(pl.multiple_of(slot * c, c), c)], sem)


def _two_slot_steps(issue, consume):
    i = pl.program_id(0)

    @pl.when(i == 0)
    def _():
        issue(0, 0)

    for slot in range(2):
        @pl.when(i % 2 == slot)
        def _():
            @pl.when(i + 1 < pl.num_programs(0))
            def _():
                issue(1, 1 - slot)

            consume(slot)


def _row_gather_body(rows_per_step, c, idx_ref, nxt_ref, src_ref, dst_ref, buf_ref, sem):
    def issue(step_offset, slot):
        ref = nxt_ref if step_offset else idx_ref

        def one(r, carry):
            _slab_copy(src_ref, ref[0, 0, r], buf_ref.at[slot], r, c, sem.at[slot]).start()
            return carry

        lax.fori_loop(0, rows_per_step, one, 0, unroll=8)

    def consume(slot):
        pltpu.make_async_copy(src_ref.at[pl.ds(0, rows_per_step * c)], buf_ref.at[slot], sem.at[slot]).wait()
        bits = _from_slabs(buf_ref.at[slot], rows_per_step, c)
        half = c * 128
        dst_ref[:, :half] = lax.bitcast_convert_type(bits & jnp.uint32(0xFFFF0000), F32).astype(dst_ref.dtype)
        dst_ref[:, half:] = lax.bitcast_convert_type(bits << 16, F32).astype(dst_ref.dtype)

    _two_slot_steps(issue, consume)


def _row_gather(src, idx, c, rows_per_step=2048):
    n_out = idx.shape[0]
    while n_out % rows_per_step:
        rows_per_step //= 2
    steps = n_out // rows_per_step
    idx3 = idx.reshape(steps, 1, rows_per_step)
    return pl.pallas_call(
        functools.partial(_row_gather_body, rows_per_step, c),
        grid=(steps,),
        in_specs=[pl.BlockSpec((1, 1, rows_per_step), lambda i: (i, 0, 0), memory_space=pltpu.SMEM),
                  pl.BlockSpec((1, 1, rows_per_step), lambda i: (jnp.minimum(i + 1, steps - 1), 0, 0),
                               memory_space=pltpu.SMEM),
                  pl.BlockSpec(memory_space=pl.ANY)],
        out_specs=pl.BlockSpec((rows_per_step, 2 * c * 128), lambda i: (i, 0)),
        out_shape=jax.ShapeDtypeStruct((n_out, 2 * c * 128), BF16),
        scratch_shapes=[pltpu.VMEM((2, rows_per_step * c, 128), src.dtype), pltpu.SemaphoreType.DMA((2,))],
        compiler_params=_params("arbitrary"),
        name="dispatch",
    )(idx3, idx3, src)


def _ffn_body(blk_e_ref, n_used_ref, x_ref, wg_ref, wu_ref, bg_ref, bu_ref, wd_ref, bd_ref, g_ref,
              y_ref, act_ref):
    i, f = pl.program_id(0), pl.program_id(1)
    n_f = pl.num_programs(1)
    used = i < n_used_ref[0]

    @pl.when(used)
    def _():
        x = x_ref[...]
        tf = wg_ref.shape[2]
        gate = jnp.dot(x, wg_ref[0], preferred_element_type=F32) + bg_ref[0]
        up = jnp.dot(x, wu_ref[0], preferred_element_type=F32) + bu_ref[0]
        gate = jnp.minimum(gate, SWIGLU_LIMIT)
        up = jnp.clip(up, -SWIGLU_LIMIT, SWIGLU_LIMIT)
        act = (up + 1.0) * gate * jax.nn.sigmoid(SWIGLU_ALPHA * gate)
        act_ref[:, pl.ds(pl.multiple_of(f * tf, tf), tf)] = act.astype(BF16)

        @pl.when(f == n_f - 1)
        def _():
            y = jnp.dot(act_ref[...], wd_ref[0], preferred_element_type=F32)
            _to_slabs(y_ref, (y + bd_ref[0]) * g_ref[...])

    @pl.when(jnp.logical_not(used) & (f == n_f - 1))
    def _():
        y_ref[...] = jnp.zeros_like(y_ref)


def _expert_ffn(xs, blk_e, n_used, w_gu, b_gu, w_down, b_down, gate_buf, tf=512):
    cap, d = xs.shape
    n_e, _, two_f = w_gu.shape
    ff = two_f // 2
    n_f = ff // tf
    n_blocks = cap // MOE_BLOCK

    def fidx(i, f, nu):
        return jnp.where(i < nu[0], f, n_f - 1)

    grid_spec = pltpu.PrefetchScalarGridSpec(
        num_scalar_prefetch=2,
        grid=(n_blocks, n_f),
        in_specs=[
            pl.BlockSpec((MOE_BLOCK, d), lambda i, f, be, nu: (jnp.minimum(i, nu[0] - 1), 0)),
            pl.BlockSpec((1, d, tf), lambda i, f, be, nu: (be[i], 0, fidx(i, f, nu))),
            pl.BlockSpec((1, d, tf), lambda i, f, be, nu: (be[i], 0, n_f + fidx(i, f, nu))),
            pl.BlockSpec((1, 1, tf), lambda i, f, be, nu: (be[i], 0, fidx(i, f, nu))),
            pl.BlockSpec((1, 1, tf), lambda i, f, be, nu: (be[i], 0, n_f + fidx(i, f, nu))),
            pl.BlockSpec((1, ff, d), lambda i, f, be, nu: (be[i], 0, 0)),
            pl.BlockSpec((1, 1, d), lambda i, f, be, nu: (be[i], 0, 0)),
            pl.BlockSpec((MOE_BLOCK, 1), lambda i, f, be, nu: (i, 0)),
        ],
        out_specs=pl.BlockSpec((MOE_BLOCK * (d // 128), 128), lambda i, f, be, nu: (i, 0)),
        scratch_shapes=[pltpu.VMEM((MOE_BLOCK, ff), BF16)],
    )
    return pl.pallas_call(
        _ffn_body,
        grid_spec=grid_spec,
        out_shape=jax.ShapeDtypeStruct((cap * (d // 128), 128), F32),
        compiler_params=_params("arbitrary", "arbitrary"),
        name="expert_ffn",
    )(blk_e, n_used, xs, w_gu, w_gu, b_gu.reshape(n_e, 1, two_f), b_gu.reshape(n_e, 1, two_f),
      w_down, b_down.reshape(n_e, 1, d), gate_buf.reshape(cap, 1))


def _combine_body(tm, c, dest_ref, nxt_ref, y_ref, x1_ref, w_ref, o_ref, buf_ref, sum_ref, sem):
    def issue(step_offset, slot):
        ref = nxt_ref if step_offset else dest_ref

        def one(t, carry):
            for k in range(TOP_K):
                _slab_copy(y_ref, ref[0, 0, t * TOP_K + k], buf_ref.at[slot, k], t, c, sem.at[slot]).start()
            return carry

        lax.fori_loop(0, tm, one, 0, unroll=2)

    def consume(slot):
        for k in range(TOP_K):
            pltpu.make_async_copy(y_ref.at[pl.ds(0, tm * c)], buf_ref.at[slot, k], sem.at[slot]).wait()
        x = x1_ref[...]
        for k in range(TOP_K):
            x = x + buf_ref[slot, k]
        x = x.reshape(tm, c, 128)
        ms = jnp.sum(jnp.sum(x * x, axis=2, keepdims=True), axis=1, keepdims=True) / (c * 128)
        sum_ref[...] = (x * lax.rsqrt(ms + NORM_EPS)).reshape(tm * c, 128)
        o_ref[...] = _from_slabs(sum_ref, tm, c) * w_ref[...]

    _two_slot_steps(issue, consume)


def _combine(y, dest, x1, w, c, tm=256):
    n = x1.shape[0] // c
    steps = n // tm
    dest3 = dest.reshape(steps, 1, tm * TOP_K)
    return pl.pallas_call(
        functools.partial(_combine_body, tm, c),
        grid=(steps,),
        in_specs=[pl.BlockSpec((1, 1, tm * TOP_K), lambda i: (i, 0, 0), memory_space=pltpu.SMEM),
                  pl.BlockSpec((1, 1, tm * TOP_K), lambda i: (jnp.minimum(i + 1, steps - 1), 0, 0),
                               memory_space=pltpu.SMEM),
                  pl.BlockSpec(memory_space=pl.ANY),
                  pl.BlockSpec((tm * c, 128), lambda i: (i, 0)),
                  pl.BlockSpec((1, c * 128), lambda i: (0, 0))],
        out_specs=pl.BlockSpec((tm, c * 128), lambda i: (i, 0)),
        out_shape=jax.ShapeDtypeStruct((n, c * 128), F32),
        scratch_shapes=[pltpu.VMEM((2, TOP_K, tm * c, 128), F32), pltpu.VMEM((tm * c, 128), F32),
                        pltpu.SemaphoreType.DMA((2,))],
        compiler_params=_params("arbitrary"),
        name="combine",
    )(dest3, dest3, y, x1, w.reshape(1, c * 128))


def _layer(x, norm_mix_w, w_in, q_norm_w, k_norm_w, out_norm_w, w_out, norm_moe_w,
           w_router, b_router, w_gate_up, b_gate_up, w_down, b_down):
    b, s, d = x.shape
    n = b * s
    x2 = x.reshape(n, d)
    b_w = d // 4
    a_q_w = d - b_w
    a_kv_w = a_q_w // GQA_RATIO

    h = _prenorm(x2, norm_mix_w)
    qT, k, vT = _aproj(h.reshape(b, s, d), w_in[:, :a_q_w], w_in[:, a_q_w:a_q_w + a_kv_w],
                       w_in[:, a_q_w + a_kv_w:a_q_w + 2 * a_kv_w], q_norm_w, k_norm_w)
    score_bound = (1.02 * HEAD_DIM ** 0.5 * LOG2E) * jnp.max(jnp.abs(q_norm_w)) * jnp.max(jnp.abs(k_norm_w))
    o_a = _attn_a(qT, k, vT, score_bound).reshape(n, a_q_w)

    o_bs, lses = [], []
    base = a_q_w + 2 * a_kv_w
    for g, (window, dilation) in enumerate(B_CONFIGS):
        w_g = w_in[:, base + 3 * g * b_w:base + 3 * (g + 1) * b_w].astype(BF16)
        o_g, lse_g = _attn_b(_bproj(h, w_g, b, dilation), window)
        o_bs.append(o_g)
        lses.append(lse_g)

    x1, hm, logits = _outproj(o_a, o_bs, lses, x2, w_out, out_norm_w, norm_moe_w, w_router, b_router)

    top_idx, gates, rank, counts = _router(logits)
    counts = counts.reshape(N_EXPERTS)
    padded = (counts + MOE_BLOCK - 1) // MOE_BLOCK * MOE_BLOCK
    pend = jnp.cumsum(padded)
    pstart = pend - padded
    dest = pstart[top_idx] + rank
    n_blocks = -(-(n * TOP_K) // MOE_BLOCK) + N_EXPERTS
    cap = n_blocks * MOE_BLOCK
    n_assign = n * TOP_K
    assign = jnp.full((cap,), n_assign, I32).at[dest.reshape(-1)].set(jnp.arange(n_assign, dtype=I32))
    tok_buf = jnp.minimum(assign // TOP_K, n - 1)
    gate_buf = jnp.concatenate([gates.reshape(-1), jnp.zeros((1,), F32)])[assign]
    blk_start = jnp.arange(n_blocks, dtype=I32) * MOE_BLOCK
    blk_e = jnp.minimum(jnp.sum((pend[None, :] <= blk_start[:, None]).astype(I32), axis=1), N_EXPERTS - 1)
    n_used = (pend[-1] // MOE_BLOCK).astype(I32).reshape(1)

    xs = _row_gather(hm, tok_buf, d // 256)
    y = _expert_ffn(xs, blk_e, n_used, w_gate_up.astype(BF16), b_gate_up, w_down.astype(BF16), b_down, gate_buf)
    return y, dest, x1


def kernel(x, norm_mix_w, w_in, q_norm_w, k_norm_w, out_norm_w, w_out, norm_moe_w, w_router, b_router,
           w_gate_up, b_gate_up, w_down, b_down, final_norm_w):
    b, s, d = x.shape
    assert w_in.shape[0] == 1, "the MoE combine is fused with the final norm: single-layer stacks only"
    y, dest, x1 = _layer(x, norm_mix_w[0], w_in[0], q_norm_w[0], k_norm_w[0], out_norm_w[0], w_out[0],
                         norm_moe_w[0], w_router[0], b_router[0], w_gate_up[0], b_gate_up[0],
                         w_down[0], b_down[0])
    return _combine(y, dest, x1, final_norm_w, d // 128).reshape(b, s, d)
```

```python
import functools

import jax
import jax.numpy as jnp
from jax import lax
from jax.experimental import pallas as pl
from jax.experimental.pallas import tpu as pltpu

F32 = jnp.float32
BF16 = jnp.bfloat16
I32 = jnp.int32

HEAD_DIM = 64
GQA_RATIO = 4
B_CONFIGS = ((128, 1), (512, 4), (2048, 16))
GRID_W = 64
ROPE_THETA = 10000.0
N_EXPERTS = 32
TOP_K = 4
MOE_BLOCK = 512
SWIGLU_LIMIT = 7.0
SWIGLU_ALPHA = 1.702
NORM_EPS = 1e-5
QK_EPS = 1e-6
LOG2E = 1.4426950408889634
NEG_BIG = -1e30
V_PAD_ROWS = 16
MAX_SAFE_LOG2_SCORE = 40.0
V7X_VMEM_LIMIT = 56 * 1024 * 1024


def _params(*sem):
    return pltpu.CompilerParams(dimension_semantics=sem, vmem_limit_bytes=V7X_VMEM_LIMIT)


def _prenorm_body(x_ref, w_ref, o_ref):
    x = x_ref[...]
    ms = jnp.mean(x * x, axis=-1, keepdims=True)
    o_ref[...] = (x * lax.rsqrt(ms + NORM_EPS) * w_ref[...]).astype(o_ref.dtype)


def _prenorm(x2, w, tm=512):
    n, d = x2.shape
    return pl.pallas_call(
        _prenorm_body,
        grid=(n // tm,),
        in_specs=[pl.BlockSpec((tm, d), lambda i: (i, 0)), pl.BlockSpec((1, d), lambda i: (0, 0))],
        out_specs=pl.BlockSpec((tm, d), lambda i: (i, 0)),
        out_shape=jax.ShapeDtypeStruct((n, d), BF16),
        compiler_params=_params("parallel"),
        name="prenorm",
    )(x2, w.reshape(1, d))


def _aproj_body(n_q, n_kv, h_ref, wqv_ref, wk_ref, qnw_ref, knw_ref, cos_ref, sin_ref,
                kc_ref, ksm_ref, ksp_ref, bd_ref, qT_ref, k_ref, vT_ref, pt_ref):
    h = h_ref[0]
    pt_ref[...] = lax.dot_general(wqv_ref[...], h, (((1,), (1,)), ((), ())),
                                  preferred_element_type=F32)
    q4 = HEAD_DIM // 4
    cr, cc = cos_ref[0:q4, :], cos_ref[q4:2 * q4, :]
    sr, sc = sin_ref[0:q4, :], sin_ref[q4:2 * q4, :]
    qnw = qnw_ref[...]

    def q_head(hh, carry):
        y = pt_ref[pl.ds(pl.multiple_of(hh * HEAD_DIM, HEAD_DIM), HEAD_DIM), :]
        ms = jnp.mean(y * y, axis=0, keepdims=True)
        y = y * lax.rsqrt(ms + QK_EPS) * qnw
        a1, a2, b1, b2 = y[0:q4], y[q4:2 * q4], y[2 * q4:3 * q4], y[3 * q4:]
        out = jnp.concatenate([a1 * cr - a2 * sr, a2 * cr + a1 * sr,
                               b1 * cc - b2 * sc, b2 * cc + b1 * sc], axis=0)
        qT_ref[0, hh] = (out * (HEAD_DIM ** -0.5 * LOG2E)).astype(BF16)
        return carry

    lax.fori_loop(0, n_q, q_head, 0)
    ones_row = (lax.broadcasted_iota(I32, (V_PAD_ROWS, pt_ref.shape[1]), 0) == 0).astype(BF16)
    for g in range(n_kv):
        lo = (n_q + g) * HEAD_DIM
        vT_ref[0, g, :HEAD_DIM] = pt_ref[lo:lo + HEAD_DIM, :].astype(BF16)
        vT_ref[0, g, HEAD_DIM:] = ones_row

    kn = jnp.dot(h, wk_ref[...], preferred_element_type=F32)
    ms = jnp.dot(kn * kn, bd_ref[...], precision=lax.Precision.HIGHEST,
                 preferred_element_type=F32)
    kn = kn * lax.rsqrt(ms + QK_EPS) * knw_ref[...]
    for c in range(n_kv // 2):
        y = kn[:, c * 128:(c + 1) * 128]
        out = (y * kc_ref[...] + pltpu.roll(y, 128 - q4, 1) * ksm_ref[...]
               + pltpu.roll(y, q4, 1) * ksp_ref[...])
        k_ref[0, 2 * c] = out[:, :HEAD_DIM].astype(BF16)
        k_ref[0, 2 * c + 1] = out[:, HEAD_DIM:].astype(BF16)


def _rope_tables(seq_len):
    rows = seq_len // GRID_W
    r, c = jnp.meshgrid(jnp.arange(rows), jnp.arange(GRID_W), indexing="ij")
    axis_dim = HEAD_DIM // 2
    inv = ROPE_THETA ** (-jnp.arange(0, axis_dim, 2, dtype=F32) / axis_dim)
    ang_r = r.reshape(-1).astype(F32)[:, None] * inv[None, :]
    ang_c = c.reshape(-1).astype(F32)[:, None] * inv[None, :]
    ang = jnp.concatenate([ang_r, ang_c], axis=-1)
    return jnp.cos(ang), jnp.sin(ang)


def _aproj(h3, w_q, w_k, w_v, q_norm_w, k_norm_w, ts=512):
    b, s, d = h3.shape
    n_q, n_kv = w_q.shape[1] // HEAD_DIM, w_k.shape[1] // HEAD_DIM
    q4 = HEAD_DIM // 4
    cos, sin = _rope_tables(s)
    zeros = jnp.zeros_like(sin[:, :q4])
    c64 = jnp.concatenate([cos[:, :q4], cos[:, :q4], cos[:, q4:], cos[:, q4:]], axis=1)
    sm64 = jnp.concatenate([-sin[:, :q4], zeros, -sin[:, q4:], zeros], axis=1)
    sp64 = jnp.concatenate([zeros, sin[:, :q4], zeros, sin[:, q4:]], axis=1)
    kc, ksm, ksp = (jnp.concatenate([t, t], axis=1) for t in (c64, sm64, sp64))
    head_of = jnp.arange(n_kv * HEAD_DIM) // HEAD_DIM
    bd = (head_of[:, None] == head_of[None, :]).astype(F32) / HEAD_DIM
    wqv_t = jnp.concatenate([w_q, w_v], axis=1).T.astype(BF16)
    rows = wqv_t.shape[0]
    full = lambda shape: pl.BlockSpec(shape, lambda bi, i: (0,) * len(shape))
    return pl.pallas_call(
        functools.partial(_aproj_body, n_q, n_kv),
        grid=(b, s // ts),
        in_specs=[
            pl.BlockSpec((1, ts, d), lambda bi, i: (bi, i, 0)),
            full((rows, d)), full((d, n_kv * HEAD_DIM)),
            full((HEAD_DIM, 1)), full((1, n_kv * HEAD_DIM)),
            pl.BlockSpec((2 * q4, ts), lambda bi, i: (0, i)),
            pl.BlockSpec((2 * q4, ts), lambda bi, i: (0, i)),
            pl.BlockSpec((ts, 128), lambda bi, i: (i, 0)),
            pl.BlockSpec((ts, 128), lambda bi, i: (i, 0)),
            pl.BlockSpec((ts, 128), lambda bi, i: (i, 0)),
            full((n_kv * HEAD_DIM, n_kv * HEAD_DIM)),
        ],
        out_specs=[
            pl.BlockSpec((1, n_q, HEAD_DIM, ts), lambda bi, i: (bi, 0, 0, i)),
            pl.BlockSpec((1, n_kv, ts, HEAD_DIM), lambda bi, i: (bi, 0, i, 0)),
            pl.BlockSpec((1, n_kv, HEAD_DIM + V_PAD_ROWS, ts), lambda bi, i: (bi, 0, 0, i)),
        ],
        out_shape=[
            jax.ShapeDtypeStruct((b, n_q, HEAD_DIM, s), BF16),
            jax.ShapeDtypeStruct((b, n_kv, s, HEAD_DIM), BF16),
            jax.ShapeDtypeStruct((b, n_kv, HEAD_DIM + V_PAD_ROWS, s), BF16),
        ],
        scratch_shapes=[pltpu.VMEM((rows, ts), F32)],
        compiler_params=_params("parallel", "parallel"),
        name="aproj",
    )(h3, wqv_t, w_k.astype(BF16), q_norm_w.reshape(HEAD_DIM, 1),
      jnp.tile(k_norm_w, n_kv).reshape(1, n_kv * HEAD_DIM), cos.T, sin.T, kc, ksm, ksp, bd)


def _attn_a_body(tk, qT_ref, k_ref, vT_ref, o_ref, q_scr, acc_scr, sa_scr, sb_scr, p_scr):
    s_len = k_ref.shape[2]
    tq = qT_ref.shape[3]
    w = GQA_RATIO * tq
    n_steps = s_len // tk
    rc = 32
    for hh in range(GQA_RATIO):
        q_scr[:, hh * tq:(hh + 1) * tq] = qT_ref[0, hh]
    acc_scr[...] = jnp.zeros(acc_scr.shape, F32)

    def scores(j, s_scr):
        off = pl.multiple_of(j * tk, tk)
        s_scr[...] = jnp.dot(k_ref[0, 0, pl.ds(off, tk), :], q_scr[...], preferred_element_type=F32)

    def half_step(j, m, s_scr, nxt_scr):
        scores(jnp.minimum(j + 1, n_steps - 1), nxt_scr)
        off = pl.multiple_of(j * tk, tk)
        v = vT_ref[0, 0, :, pl.ds(off, tk)]
        m8 = jnp.full((8, w), NEG_BIG, F32)
        for c in range(tk // rc):
            sc = s_scr[c * rc:(c + 1) * rc, :]
            m8 = jnp.maximum(m8, jnp.max(sc.reshape(rc // 8, 8, w), axis=0))
        m_new = jnp.maximum(m, jnp.max(m8, axis=0, keepdims=True))
        for c in range(tk // rc):
            sc = s_scr[c * rc:(c + 1) * rc, :]
            p_scr[c * rc:(c + 1) * rc, :] = jnp.exp2(sc - m_new).astype(BF16)
        alpha = jnp.exp2(m - m_new)
        acc_scr[...] = alpha * acc_scr[...] + jnp.dot(v, p_scr[...], preferred_element_type=F32)
        return m_new

    def kv_step(jj, m):
        m = half_step(2 * jj, m, sa_scr, sb_scr)
        return half_step(2 * jj + 1, m, sb_scr, sa_scr)

    scores(0, sa_scr)
    lax.fori_loop(0, n_steps // 2, kv_step, jnp.full((1, w), NEG_BIG, F32))
    o = acc_scr[:HEAD_DIM, :] / acc_scr[HEAD_DIM:HEAD_DIM + 1, :]
    for hh in range(GQA_RATIO):
        o_ref[0, :, hh * HEAD_DIM:(hh + 1) * HEAD_DIM] = o[:, hh * tq:(hh + 1) * tq].T.astype(o_ref.dtype)


def _attn_a_bounded_body(tk, qT_ref, k_ref, vT_ref, o_ref, q_scr, pa_scr, pb_scr, acc_scr):
    s_len = k_ref.shape[2]
    tq = qT_ref.shape[3]
    n_steps = s_len // tk
    for hh in range(GQA_RATIO):
        q_scr[:, hh * tq:(hh + 1) * tq] = qT_ref[0, hh]
    acc_scr[...] = jnp.zeros(acc_scr.shape, F32)

    def chunk(j):
        return pl.ds(j * tk if isinstance(j, int) else pl.multiple_of(j * tk, tk), tk)

    def probs(j, p_scr):
        s = jnp.dot(k_ref[0, 0, chunk(j), :], q_scr[...], preferred_element_type=F32)
        p_scr[...] = jnp.exp2(s).astype(BF16)

    def half_step(j, p_scr, nxt_scr):
        if nxt_scr is not None:
            probs(j + 1, nxt_scr)
        acc_scr[...] += jnp.dot(vT_ref[0, 0, :, chunk(j)], p_scr[...], preferred_element_type=F32)

    def kv_step(jj, carry):
        half_step(2 * jj, pa_scr, pb_scr)
        half_step(2 * jj + 1, pb_scr, pa_scr)
        return carry

    probs(0, pa_scr)
    lax.fori_loop(0, n_steps // 2 - 1, kv_step, 0)
    half_step(n_steps - 2, pa_scr, pb_scr)
    half_step(n_steps - 1, pb_scr, None)
    o = acc_scr[:HEAD_DIM, :] / acc_scr[HEAD_DIM:HEAD_DIM + 1, :]
    for hh in range(GQA_RATIO):
        o_ref[0, :, hh * HEAD_DIM:(hh + 1) * HEAD_DIM] = o[:, hh * tq:(hh + 1) * tq].T.astype(o_ref.dtype)


def _attn_a_call(body, tq, scratch, name, qT, k, vT):
    b, n_q, _, s = qT.shape
    n_kv, vr = k.shape[1], vT.shape[2]
    return pl.pallas_call(
        body,
        grid=(b, n_kv, s // tq),
        in_specs=[
            pl.BlockSpec((1, GQA_RATIO, HEAD_DIM, tq), lambda bi, g, i: (bi, g, 0, i)),
            pl.BlockSpec((1, 1, s, HEAD_DIM), lambda bi, g, i: (bi, g, 0, 0)),
            pl.BlockSpec((1, 1, vr, s), lambda bi, g, i: (bi, g, 0, 0)),
        ],
        out_specs=pl.BlockSpec((1, tq, GQA_RATIO * HEAD_DIM), lambda bi, g, i: (bi, i, g)),
        out_shape=jax.ShapeDtypeStruct((b, s, n_q * HEAD_DIM), BF16),
        scratch_shapes=scratch,
        compiler_params=_params("parallel", "parallel", "parallel"),
        name=name,
    )(qT, k, vT)


def _attn_a(qT, k, vT, score_bound, tq=256, tk=256, tq_bounded=1024, tk_bounded=512):
    s = qT.shape[3]
    vr = vT.shape[2]
    tq_bounded, tk_bounded = min(tq_bounded, s), min(tk_bounded, s // 2)
    assert s % (2 * tk) == 0 and s % tq == 0 and s % (2 * tk_bounded) == 0 and s % tq_bounded == 0
    w, wb = GQA_RATIO * tq, GQA_RATIO * tq_bounded
    general = functools.partial(
        _attn_a_call, functools.partial(_attn_a_body, tk), tq,
        [pltpu.VMEM((HEAD_DIM, w), BF16), pltpu.VMEM((vr, w), F32),
         pltpu.VMEM((tk, w), F32), pltpu.VMEM((tk, w), F32), pltpu.VMEM((tk, w), BF16)], "attn_a")
    bounded = functools.partial(
        _attn_a_call, functools.partial(_attn_a_bounded_body, tk_bounded), tq_bounded,
        [pltpu.VMEM((HEAD_DIM, wb), BF16), pltpu.VMEM((tk_bounded, wb), BF16),
         pltpu.VMEM((tk_bounded, wb), BF16), pltpu.VMEM((vr, wb), F32)], "attn_a_bounded")
    return lax.cond(score_bound <= MAX_SAFE_LOG2_SCORE, bounded, general, qT, k, vT)


def _bproj_body(dilation, h_ref, w_ref, o_ref, scr):
    res = jnp.dot(h_ref[...], w_ref[...], preferred_element_type=F32)
    if dilation == 1:
        o_ref[0] = res.astype(o_ref.dtype)
    else:
        rows = scr.shape[1] // dilation
        for c in range(scr.shape[0]):
            scr[c] = res[:, c * 128:(c + 1) * 128]
        for r in range(dilation):
            for c in range(scr.shape[0]):
                o_ref[r, :, c * 128:(c + 1) * 128] = scr[c, pl.ds(r, rows, stride=dilation), :].astype(o_ref.dtype)


def _bproj(h, w, batch, dilation, tm=512):
    n, d = h.shape
    e = w.shape[1]
    s = n // batch
    per_b = s // tm
    assert tm % (16 * dilation) == 0
    return pl.pallas_call(
        functools.partial(_bproj_body, dilation),
        grid=(n // tm,),
        in_specs=[pl.BlockSpec((tm, d), lambda i: (i, 0)), pl.BlockSpec((d, e), lambda i: (0, 0))],
        out_specs=pl.BlockSpec((None, dilation, tm // dilation, e),
                               lambda i: (i // per_b, 0, i % per_b, 0)),
        out_shape=jax.ShapeDtypeStruct((batch, dilation, s // dilation, e), BF16),
        scratch_shapes=[pltpu.VMEM((e // 128, tm, 128), F32)],
        compiler_params=_params("parallel"),
        name=f"bproj_d{dilation}",
    )(h, w)


def _attn_b_body(dilation, radius, q_ref, kp_ref, kc_ref, kn_ref, vp_ref, vc_ref, vn_ref,
                 o_ref, lse_ref):
    tu = q_ref.shape[0]
    n_pairs = q_ref.shape[1] // 128
    i = pl.program_id(2)
    n_u = pl.num_programs(2) * tu
    row = lax.broadcasted_iota(I32, (tu, 2 * tu), 0)
    col = lax.broadcasted_iota(I32, (tu, 2 * tu), 1)
    j = col - radius - row
    key = i * tu - radius + col
    valid = (jnp.abs(j) <= radius) & (key >= 0) & (key < n_u)
    dist = (dilation * jnp.abs(j)).astype(F32)
    pen = jnp.where(valid, 0.0, NEG_BIG)
    first = lax.broadcasted_iota(I32, (1, 128), 1) < HEAD_DIM
    scale = HEAD_DIM ** -0.5
    n_heads = 2 * n_pairs
    for hp in range(n_pairs):
        lanes = slice(hp * 128, (hp + 1) * 128)
        q2 = q_ref[:, lanes]
        kw = jnp.concatenate([kp_ref[tu - radius:, lanes], kc_ref[:, lanes],
                              kn_ref[:tu - radius, lanes]], axis=0)
        vw = jnp.concatenate([vp_ref[tu - radius:, lanes], vc_ref[:, lanes],
                              vn_ref[:tu - radius, lanes]], axis=0)
        o2 = jnp.zeros((tu, 128), F32)
        lse2 = jnp.zeros((tu, 128), F32)
        for sub in range(2):
            sel = first if sub == 0 else jnp.logical_not(first)
            slope = 2.0 ** (-8.0 * (2 * hp + sub + 1) / n_heads)
            qm = jnp.where(sel, q2, jnp.zeros_like(q2))
            s = lax.dot_general(qm, kw, (((1,), (1,)), ((), ())), preferred_element_type=F32)
            s = s * scale + (pen - slope * dist)
            m = jnp.max(s, axis=1, keepdims=True)
            p = jnp.exp(s - m)
            l = jnp.sum(p, axis=1, keepdims=True)
            vm = jnp.where(sel, vw, jnp.zeros_like(vw))
            o2 = o2 + jnp.dot(p.astype(BF16), vm, preferred_element_type=F32) / l
            lse2 = jnp.where(sel, m + jnp.log(l), lse2)
        o_ref[:, lanes] = o2.astype(o_ref.dtype)
        lse_ref[:, lanes] = lse2


def _attn_b(pbd, window, tu=128):
    b, dilation, u, e = pbd.shape
    width = e // 3
    radius = window // (2 * dilation)
    assert radius < tu and u % tu == 0
    n_t = u // tu

    def spec(which, shift):
        def index(bi, r, i):
            return (bi, r, jnp.clip(i + shift, 0, n_t - 1), which)
        return pl.BlockSpec((None, None, tu, width), index)

    out_spec = pl.BlockSpec((None, None, tu, width), lambda bi, r, i: (bi, r, i, 0))
    return pl.pallas_call(
        functools.partial(_attn_b_body, dilation, radius),
        grid=(b, dilation, n_t),
        in_specs=[spec(0, 0), spec(1, -1), spec(1, 0), spec(1, 1), spec(2, -1), spec(2, 0), spec(2, 1)],
        out_specs=[out_spec, out_spec],
        out_shape=[jax.ShapeDtypeStruct((b, dilation, u, width), BF16),
                   jax.ShapeDtypeStruct((b, dilation, u, width), F32)],
        compiler_params=_params("parallel", "parallel", "parallel"),
        name=f"attn_b_d{dilation}",
    )(pbd, pbd, pbd, pbd, pbd, pbd, pbd)


def _rms(x, w):
    ms = jnp.mean(x * x, axis=-1, keepdims=True)
    return x * lax.rsqrt(ms + NORM_EPS) * w


def _to_slabs(ref, x):
    rows, width = x.shape
    c = width // 128
    for j in range(c):
        ref[pl.ds(j, rows, stride=c), :] = x[:, j * 128:(j + 1) * 128]


def _from_slabs(ref, rows, c):
    return jnp.concatenate([ref[pl.ds(j, rows, stride=c), :] for j in range(c)], axis=1)


def _token_order(ref, scr):
    d, rows, _ = ref.shape
    if d == 1:
        return ref[0].astype(F32)
    for r in range(d):
        slab = ref[r].astype(F32)
        for c in range(scr.shape[0]):
            scr[c, pl.ds(r, rows, stride=d), :] = slab[:, c * 128:(c + 1) * 128]
    return jnp.concatenate([scr[c] for c in range(scr.shape[0])], axis=1)


def _outproj_body(n_groups, *refs):
    oa_ref = refs[0]
    ob_refs = refs[1:1 + n_groups]
    lse_refs = refs[1 + n_groups:1 + 2 * n_groups]
    (x_ref, wo_a_ref, wo_b_ref, nwa_ref, nwb_ref, nm_ref, wr_ref, br_ref,
     x1_ref, hm_ref, lg_ref) = refs[1 + 2 * n_groups:12 + 2 * n_groups]
    scrs = refs[12 + 2 * n_groups:]
    obs = [_token_order(r, scrs[2 * g]) for g, r in enumerate(ob_refs)]
    lses = [_token_order(r, scrs[2 * g + 1]) for g, r in enumerate(lse_refs)]
    mx = functools.reduce(jnp.maximum, lses)
    ws = [jnp.exp(v - mx) for v in lses]
    den = functools.reduce(lambda a, c: a + c, ws)
    ob = functools.reduce(lambda a, c: a + c, [w * o for w, o in zip(ws, obs)]) / den
    na = _rms(oa_ref[...].astype(F32), nwa_ref[...]).astype(BF16)
    nb = _rms(ob, nwb_ref[...]).astype(BF16)
    x1 = (x_ref[...] + jnp.dot(na, wo_a_ref[...], preferred_element_type=F32)
          + jnp.dot(nb, wo_b_ref[...], preferred_element_type=F32))
    _to_slabs(x1_ref, x1)
    hm = _rms(x1, nm_ref[...])
    hm_hi = hm.astype(BF16)
    half = hm.shape[1] // 2
    bits = lax.bitcast_convert_type(hm_hi.astype(F32), jnp.uint32)
    _to_slabs(hm_ref, bits[:, :half] | (bits[:, half:] >> 16))
    hm_lo = (hm - hm_hi.astype(F32)).astype(BF16)
    lg_ref[...] = (jnp.dot(hm_hi, wr_ref[0], preferred_element_type=F32)
                   + jnp.dot(hm_lo, wr_ref[0], preferred_element_type=F32)
                   + jnp.dot(hm_hi, wr_ref[1], preferred_element_type=F32)) + br_ref[...]


def _outproj(o_a, o_bs, lses, x2, w_out, out_norm_w, norm_moe_w, w_router, b_router, tm=256):
    n, d = x2.shape
    wa, wb = o_a.shape[1], o_bs[0].shape[3]
    n_e = w_router.shape[1]
    per_b = n // o_bs[0].shape[0] // tm
    row = lambda width: pl.BlockSpec((tm, width), lambda i: (i, 0))
    full = lambda shape: pl.BlockSpec(shape, lambda i: (0,) * len(shape))
    grouped = [pl.BlockSpec((None, o.shape[1], tm // o.shape[1], wb),
                            lambda i: (i // per_b, 0, i % per_b, 0)) for o in o_bs]
    n_groups = len(o_bs)
    c = d // 128
    slabs = pl.BlockSpec((tm * c, 128), lambda i: (i, 0))
    wr_hi = w_router.astype(BF16)
    return pl.pallas_call(
        functools.partial(_outproj_body, n_groups),
        grid=(n // tm,),
        in_specs=[row(wa)] + grouped + grouped + [
            row(d), full((wa, d)), full((wb, d)), full((1, wa)), full((1, wb)), full((1, d)),
            full((2, d, n_e)), full((1, n_e))],
        out_specs=[slabs, pl.BlockSpec((tm * c // 2, 128), lambda i: (i, 0)), row(n_e)],
        out_shape=[jax.ShapeDtypeStruct((n * c, 128), F32), jax.ShapeDtypeStruct((n * c // 2, 128), jnp.uint32),
                   jax.ShapeDtypeStruct((n, n_e), F32)],
        scratch_shapes=[pltpu.VMEM((wb // 128, tm, 128), F32)] * (2 * n_groups),
        compiler_params=_params("parallel"),
        name="outproj",
    )(o_a, *o_bs, *lses, x2, w_out[:wa].astype(BF16), w_out[wa:].astype(BF16),
      out_norm_w[:wa].reshape(1, wa), out_norm_w[wa:].reshape(1, wb), norm_moe_w.reshape(1, d),
      jnp.stack([wr_hi, (w_router - wr_hi.astype(F32)).astype(BF16)]), b_router.reshape(1, n_e))


def _router_body(lg_ref, idx_ref, gate_ref, rank_ref, cnt_ref, carry_ref):
    i = pl.program_id(0)
    tm, n_e = lg_ref.shape

    @pl.when(i == 0)
    def _():
        carry_ref[...] = jnp.zeros_like(carry_ref)

    lane = lax.broadcasted_iota(I32, (tm, n_e), 1)
    work = lg_ref[...]
    vals, idxs = [], []
    for _ in range(TOP_K):
        mx = jnp.max(work, axis=1, keepdims=True)
        ix = jnp.min(jnp.where(work == mx, lane, n_e), axis=1, keepdims=True)
        vals.append(mx)
        idxs.append(ix)
        work = jnp.where(lane == ix, -jnp.inf, work)
    es = [jnp.exp(v - vals[0]) for v in vals]
    den = functools.reduce(lambda a, c: a + c, es)
    chosen = functools.reduce(lambda a, c: a + c, [(lane == ix).astype(F32) for ix in idxs])
    r = lax.broadcasted_iota(I32, (tm, tm), 0)
    c = lax.broadcasted_iota(I32, (tm, tm), 1)
    before = jnp.where(c < r, 1.0, 0.0).astype(BF16)
    prefix = jnp.dot(before, chosen.astype(BF16), preferred_element_type=F32) + carry_ref[...]
    for k in range(TOP_K):
        idx_ref[:, k:k + 1] = idxs[k]
        gate_ref[:, k:k + 1] = es[k] / den
        rank_ref[:, k:k + 1] = jnp.sum(jnp.where(lane == idxs[k], prefix, 0.0), axis=1,
                                       keepdims=True).astype(I32)
    carry_ref[...] += jnp.sum(chosen, axis=0, keepdims=True)
    cnt_ref[...] = carry_ref[...].astype(I32)


def _router(logits, tm=256):
    n, n_e = logits.shape
    row = pl.BlockSpec((tm, TOP_K), lambda i: (i, 0))
    return pl.pallas_call(
        _router_body,
        grid=(n // tm,),
        in_specs=[pl.BlockSpec((tm, n_e), lambda i: (i, 0))],
        out_specs=[row, row, row, pl.BlockSpec((1, n_e), lambda i: (0, 0))],
        out_shape=[jax.ShapeDtypeStruct((n, TOP_K), I32), jax.ShapeDtypeStruct((n, TOP_K), F32),
                   jax.ShapeDtypeStruct((n, TOP_K), I32), jax.ShapeDtypeStruct((1, n_e), I32)],
        scratch_shapes=[pltpu.VMEM((1, n_e), F32)],
        compiler_params=_params("arbitrary"),
        name="router",
    )(logits)


def _slab_copy(src_ref, row, buf_ref, slot, c, sem):
    return pltpu.make_async_copy(src_ref.at[pl.ds(pl.multiple_of(row * c, c), c)],
                                 buf_ref.at[pl.ds(pl.multiple_of(slot * c, c), c)], sem)


def _two_slot_steps(issue, consume):
    i = pl.program_id(0)

    @pl.when(i == 0)
    def _():
        issue(0, 0)

    for slot in range(2):
        @pl.when(i % 2 == slot)
        def _():
            @pl.when(i + 1 < pl.num_programs(0))
            def _():
                issue(1, 1 - slot)

            consume(slot)


def _row_gather_body(rows_per_step, c, idx_ref, nxt_ref, src_ref, dst_ref, buf_ref, sem):
    def issue(step_offset, slot):
        ref = nxt_ref if step_offset else idx_ref

        def one(r, carry):
            _slab_copy(src_ref, ref[0, 0, r], buf_ref.at[slot], r, c, sem.at[slot]).start()
            return carry

        lax.fori_loop(0, rows_per_step, one, 0, unroll=8)

    def consume(slot):
        pltpu.make_async_copy(src_ref.at[pl.ds(0, rows_per_step * c)], buf_ref.at[slot], sem.at[slot]).wait()
        bits = _from_slabs(buf_ref.at[slot], rows_per_step, c)
        half = c * 128
        dst_ref[:, :half] = lax.bitcast_convert_type(bits & jnp.uint32(0xFFFF0000), F32).astype(dst_ref.dtype)
        dst_ref[:, half:] = lax.bitcast_convert_type(bits << 16, F32).astype(dst_ref.dtype)

    _two_slot_steps(issue, consume)


def _row_gather(src, idx, c, rows_per_step=2048):
    n_out = idx.shape[0]
    while n_out % rows_per_step:
        rows_per_step //= 2
    steps = n_out // rows_per_step
    idx3 = idx.reshape(steps, 1, rows_per_step)
    return pl.pallas_call(
        functools.partial(_row_gather_body, rows_per_step, c),
        grid=(steps,),
        in_specs=[pl.BlockSpec((1, 1, rows_per_step), lambda i: (i, 0, 0), memory_space=pltpu.SMEM),
                  pl.BlockSpec((1, 1, rows_per_step), lambda i: (jnp.minimum(i + 1, steps - 1), 0, 0),
                               memory_space=pltpu.SMEM),
                  pl.BlockSpec(memory_space=pl.ANY)],
        out_specs=pl.BlockSpec((rows_per_step, 2 * c * 128), lambda i: (i, 0)),
        out_shape=jax.ShapeDtypeStruct((n_out, 2 * c * 128), BF16),
        scratch_shapes=[pltpu.VMEM((2, rows_per_step * c, 128), src.dtype), pltpu.SemaphoreType.DMA((2,))],
        compiler_params=_params("arbitrary"),
        name="dispatch",
    )(idx3, idx3, src)


def _ffn_body(blk_e_ref, n_used_ref, x_ref, wg_ref, wu_ref, bg_ref, bu_ref, wd_ref, bd_ref, g_ref,
              y_ref, act_ref):
    i, f = pl.program_id(0), pl.program_id(1)
    n_f = pl.num_programs(1)
    used = i < n_used_ref[0]

    @pl.when(used)
    def _():
        x = x_ref[...]
        tf = wg_ref.shape[2]
        gate = jnp.dot(x, wg_ref[0], preferred_element_type=F32) + bg_ref[0]
        up = jnp.dot(x, wu_ref[0], preferred_element_type=F32) + bu_ref[0]
        gate = jnp.minimum(gate, SWIGLU_LIMIT)
        up = jnp.clip(up, -SWIGLU_LIMIT, SWIGLU_LIMIT)
        act = (up + 1.0) * gate * jax.nn.sigmoid(SWIGLU_ALPHA * gate)
        act_ref[:, pl.ds(pl.multiple_of(f * tf, tf), tf)] = act.astype(BF16)

        @pl.when(f == n_f - 1)
        def _():
            y = jnp.dot(act_ref[...], wd_ref[0], preferred_element_type=F32)
            _to_slabs(y_ref, (y + bd_ref[0]) * g_ref[...])

    @pl.when(jnp.logical_not(used) & (f == n_f - 1))
    def _():
        y_ref[...] = jnp.zeros_like(y_ref)


def _expert_ffn(xs, blk_e, n_used, w_gu, b_gu, w_down, b_down, gate_buf, tf=512):
    cap, d = xs.shape
    n_e, _, two_f = w_gu.shape
    ff = two_f // 2
    n_f = ff // tf
    n_blocks = cap // MOE_BLOCK

    def fidx(i, f, nu):
        return jnp.where(i < nu[0], f, n_f - 1)

    grid_spec = pltpu.PrefetchScalarGridSpec(
        num_scalar_prefetch=2,
        grid=(n_blocks, n_f),
        in_specs=[
            pl.BlockSpec((MOE_BLOCK, d), lambda i, f, be, nu: (jnp.minimum(i, nu[0] - 1), 0)),
            pl.BlockSpec((1, d, tf), lambda i, f, be, nu: (be[i], 0, fidx(i, f, nu))),
            pl.BlockSpec((1, d, tf), lambda i, f, be, nu: (be[i], 0, n_f + fidx(i, f, nu))),
            pl.BlockSpec((1, 1, tf), lambda i, f, be, nu: (be[i], 0, fidx(i, f, nu))),
            pl.BlockSpec((1, 1, tf), lambda i, f, be, nu: (be[i], 0, n_f + fidx(i, f, nu))),
            pl.BlockSpec((1, ff, d), lambda i, f, be, nu: (be[i], 0, 0)),
            pl.BlockSpec((1, 1, d), lambda i, f, be, nu: (be[i], 0, 0)),
            pl.BlockSpec((MOE_BLOCK, 1), lambda i, f, be, nu: (i, 0)),
        ],
        out_specs=pl.BlockSpec((MOE_BLOCK * (d // 128), 128), lambda i, f, be, nu: (i, 0)),
        scratch_shapes=[pltpu.VMEM((MOE_BLOCK, ff), BF16)],
    )
    return pl.pallas_call(
        _ffn_body,
        grid_spec=grid_spec,
        out_shape=jax.ShapeDtypeStruct((cap * (d // 128), 128), F32),
        compiler_params=_params("arbitrary", "arbitrary"),
        name="expert_ffn",
    )(blk_e, n_used, xs, w_gu, w_gu, b_gu.reshape(n_e, 1, two_f), b_gu.reshape(n_e, 1, two_f),
      w_down, b_down.reshape(n_e, 1, d), gate_buf.reshape(cap, 1))


def _combine_body(tm, c, dest_ref, nxt_ref, y_ref, x1_ref, w_ref, o_ref, buf_ref, sum_ref, sem):
    def issue(step_offset, slot):
        ref = nxt_ref if step_offset else dest_ref

        def one(t, carry):
            for k in range(TOP_K):
                _slab_copy(y_ref, ref[0, 0, t * TOP_K + k], buf_ref.at[slot, k], t, c, sem.at[slot]).start()
            return carry

        lax.fori_loop(0, tm, one, 0, unroll=2)

    def consume(slot):
        for k in range(TOP_K):
            pltpu.make_async_copy(y_ref.at[pl.ds(0, tm * c)], buf_ref.at[slot, k], sem.at[slot]).wait()
        x = x1_ref[...]
        for k in range(TOP_K):
            x = x + buf_ref[slot, k]
        x = x.reshape(tm, c, 128)
        ms = jnp.sum(jnp.sum(x * x, axis=2, keepdims=True), axis=1, keepdims=True) / (c * 128)
        sum_ref[...] = (x * lax.rsqrt(ms + NORM_EPS)).reshape(tm * c, 128)
        o_ref[...] = _from_slabs(sum_ref, tm, c) * w_ref[...]

    _two_slot_steps(issue, consume)


def _combine(y, dest, x1, w, c, tm=256):
    n = x1.shape[0] // c
    steps = n // tm
    dest3 = dest.reshape(steps, 1, tm * TOP_K)
    return pl.pallas_call(
        functools.partial(_combine_body, tm, c),
        grid=(steps,),
        in_specs=[pl.BlockSpec((1, 1, tm * TOP_K), lambda i: (i, 0, 0), memory_space=pltpu.SMEM),
                  pl.BlockSpec((1, 1, tm * TOP_K), lambda i: (jnp.minimum(i + 1, steps - 1), 0, 0),
                               memory_space=pltpu.SMEM),
                  pl.BlockSpec(memory_space=pl.ANY),
                  pl.BlockSpec((tm * c, 128), lambda i: (i, 0)),
                  pl.BlockSpec((1, c * 128), lambda i: (0, 0))],
        out_specs=pl.BlockSpec((tm, c * 128), lambda i: (i, 0)),
        out_shape=jax.ShapeDtypeStruct((n, c * 128), F32),
        scratch_shapes=[pltpu.VMEM((2, TOP_K, tm * c, 128), F32), pltpu.VMEM((tm * c, 128), F32),
                        pltpu.SemaphoreType.DMA((2,))],
        compiler_params=_params("arbitrary"),
        name="combine",
    )(dest3, dest3, y, x1, w.reshape(1, c * 128))


def _layer(x, norm_mix_w, w_in, q_norm_w, k_norm_w, out_norm_w, w_out, norm_moe_w,
           w_router, b_router, w_gate_up, b_gate_up, w_down, b_down):
    b, s, d = x.shape
    n = b * s
    x2 = x.reshape(n, d)
    b_w = d // 4
    a_q_w = d - b_w
    a_kv_w = a_q_w // GQA_RATIO

    h = _prenorm(x2, norm_mix_w)
    qT, k, vT = _aproj(h.reshape(b, s, d), w_in[:, :a_q_w], w_in[:, a_q_w:a_q_w + a_kv_w],
                       w_in[:, a_q_w + a_kv_w:a_q_w + 2 * a_kv_w], q_norm_w, k_norm_w)
    score_bound = (1.02 * HEAD_DIM ** 0.5 * LOG2E) * jnp.max(jnp.abs(q_norm_w)) * jnp.max(jnp.abs(k_norm_w))
    o_a = _attn_a(qT, k, vT, score_bound).reshape(n, a_q_w)

    o_bs, lses = [], []
    base = a_q_w + 2 * a_kv_w
    for g, (window, dilation) in enumerate(B_CONFIGS):
        w_g = w_in[:, base + 3 * g * b_w:base + 3 * (g + 1) * b_w].astype(BF16)
        o_g, lse_g = _attn_b(_bproj(h, w_g, b, dilation), window)
        o_bs.append(o_g)
        lses.append(lse_g)

    x1, hm, logits = _outproj(o_a, o_bs, lses, x2, w_out, out_norm_w, norm_moe_w, w_router, b_router)

    top_idx, gates, rank, counts = _router(logits)
    counts = counts.reshape(N_EXPERTS)
    padded = (counts + MOE_BLOCK - 1) // MOE_BLOCK * MOE_BLOCK
    pend = jnp.cumsum(padded)
    pstart = pend - padded
    dest = pstart[top_idx] + rank
    n_blocks = -(-(n * TOP_K) // MOE_BLOCK) + N_EXPERTS
    cap = n_blocks * MOE_BLOCK
    n_assign = n * TOP_K
    assign = jnp.full((cap,), n_assign, I32).at[dest.reshape(-1)].set(jnp.arange(n_assign, dtype=I32))
    tok_buf = jnp.minimum(assign // TOP_K, n - 1)
    gate_buf = jnp.concatenate([gates.reshape(-1), jnp.zeros((1,), F32)])[assign]
    blk_start = jnp.arange(n_blocks, dtype=I32) * MOE_BLOCK
    blk_e = jnp.minimum(jnp.sum((pend[None, :] <= blk_start[:, None]).astype(I32), axis=1), N_EXPERTS - 1)
    n_used = (pend[-1] // MOE_BLOCK).astype(I32).reshape(1)

    xs = _row_gather(hm, tok_buf, d // 256)
    y = _expert_ffn(xs, blk_e, n_used, w_gate_up.astype(BF16), b_gate_up, w_down.astype(BF16), b_down, gate_buf)
    return y, dest, x1


def kernel(x, norm_mix_w, w_in, q_norm_w, k_norm_w, out_norm_w, w_out, norm_moe_w, w_router, b_router,
           w_gate_up, b_gate_up, w_down, b_down, final_norm_w):
    b, s, d = x.shape
    assert w_in.shape[0] == 1, "the MoE combine is fused with the final norm: single-layer stacks only"
    y, dest, x1 = _layer(x, norm_mix_w[0], w_in[0], q_norm_w[0], k_norm_w[0], out_norm_w[0], w_out[0],
                         norm_moe_w[0], w_router[0], b_router[0], w_gate_up[0], b_gate_up[0],
                         w_down[0], b_down[0])
    return _combine(y, dest, x1, final_norm_w, d // 128).reshape(b, s, d)
```

```python
import functools

import jax
import jax.numpy as jnp
from jax import lax
from jax.experimental import pallas as pl
from jax.experimental.pallas import tpu as pltpu

F32 = jnp.float32
BF16 = jnp.bfloat16
I32 = jnp.int32

HEAD_DIM = 64
GQA_RATIO = 4
B_CONFIGS = ((128, 1), (512, 4), (2048, 16))
GRID_W = 64
ROPE_THETA = 10000.0
N_EXPERTS = 32
TOP_K = 4
MOE_BLOCK = 512
SWIGLU_LIMIT = 7.0
SWIGLU_ALPHA = 1.702
NORM_EPS = 1e-5
QK_EPS = 1e-6
LOG2E = 1.4426950408889634
NEG_BIG = -1e30
V_PAD_ROWS = 16
MAX_SAFE_LOG2_SCORE = 40.0
V7X_VMEM_LIMIT = 56 * 1024 * 1024


def _params(*sem):
    return pltpu.CompilerParams(dimension_semantics=sem, vmem_limit_bytes=V7X_VMEM_LIMIT)


def _prenorm_body(x_ref, w_ref, o_ref):
    x = x_ref[...]
    ms = jnp.mean(x * x, axis=-1, keepdims=True)
    o_ref[...] = (x * lax.rsqrt(ms + NORM_EPS) * w_ref[...]).astype(o_ref.dtype)


def _prenorm(x2, w, tm=512):
    n, d = x2.shape
    return pl.pallas_call(
        _prenorm_body,
        grid=(n // tm,),
        in_specs=[pl.BlockSpec((tm, d), lambda i: (i, 0)), pl.BlockSpec((1, d), lambda i: (0, 0))],
        out_specs=pl.BlockSpec((tm, d), lambda i: (i, 0)),
        out_shape=jax.ShapeDtypeStruct((n, d), BF16),
        compiler_params=_params("parallel"),
        name="prenorm",
    )(x2, w.reshape(1, d))


def _aproj_body(n_q, n_kv, h_ref, wqv_ref, wk_ref, qnw_ref, knw_ref, cos_ref, sin_ref,
                kc_ref, ksm_ref, ksp_ref, bd_ref, qT_ref, k_ref, vT_ref, pt_ref):
    h = h_ref[0]
    pt_ref[...] = lax.dot_general(wqv_ref[...], h, (((1,), (1,)), ((), ())),
                                  preferred_element_type=F32)
    q4 = HEAD_DIM // 4
    cr, cc = cos_ref[0:q4, :], cos_ref[q4:2 * q4, :]
    sr, sc = sin_ref[0:q4, :], sin_ref[q4:2 * q4, :]
    qnw = qnw_ref[...]

    def q_head(hh, carry):
        y = pt_ref[pl.ds(pl.multiple_of(hh * HEAD_DIM, HEAD_DIM), HEAD_DIM), :]
        ms = jnp.mean(y * y, axis=0, keepdims=True)
        y = y * lax.rsqrt(ms + QK_EPS) * qnw
        a1, a2, b1, b2 = y[0:q4], y[q4:2 * q4], y[2 * q4:3 * q4], y[3 * q4:]
        out = jnp.concatenate([a1 * cr - a2 * sr, a2 * cr + a1 * sr,
                               b1 * cc - b2 * sc, b2 * cc + b1 * sc], axis=0)
        qT_ref[0, hh] = (out * (HEAD_DIM ** -0.5 * LOG2E)).astype(BF16)
        return carry

    lax.fori_loop(0, n_q, q_head, 0)
    ones_row = (lax.broadcasted_iota(I32, (V_PAD_ROWS, pt_ref.shape[1]), 0) == 0).astype(BF16)
    for g in range(n_kv):
        lo = (n_q + g) * HEAD_DIM
        vT_ref[0, g, :HEAD_DIM] = pt_ref[lo:lo + HEAD_DIM, :].astype(BF16)
        vT_ref[0, g, HEAD_DIM:] = ones_row

    kn = jnp.dot(h, wk_ref[...], preferred_element_type=F32)
    sq = kn * kn
    sq_hi = sq.astype(BF16)
    sq_lo = (sq - sq_hi.astype(F32)).astype(BF16)
    ms = (jnp.dot(sq_hi, bd_ref[...], preferred_element_type=F32)
          + jnp.dot(sq_lo, bd_ref[...], preferred_element_type=F32)) * (1.0 / HEAD_DIM)
    kn = kn * lax.rsqrt(ms + QK_EPS) * knw_ref[...]
    for c in range(n_kv // 2):
        y = kn[:, c * 128:(c + 1) * 128]
        out = (y * kc_ref[...] + pltpu.roll(y, 128 - q4, 1) * ksm_ref[...]
               + pltpu.roll(y, q4, 1) * ksp_ref[...])
        k_ref[0, 2 * c] = out[:, :HEAD_DIM].astype(BF16)
        k_ref[0, 2 * c + 1] = out[:, HEAD_DIM:].astype(BF16)


def _rope_tables(seq_len):
    rows = seq_len // GRID_W
    r, c = jnp.meshgrid(jnp.arange(rows), jnp.arange(GRID_W), indexing="ij")
    axis_dim = HEAD_DIM // 2
    inv = ROPE_THETA ** (-jnp.arange(0, axis_dim, 2, dtype=F32) / axis_dim)
    ang_r = r.reshape(-1).astype(F32)[:, None] * inv[None, :]
    ang_c = c.reshape(-1).astype(F32)[:, None] * inv[None, :]
    ang = jnp.concatenate([ang_r, ang_c], axis=-1)
    return jnp.cos(ang), jnp.sin(ang)


def _aproj(h3, w_q, w_k, w_v, q_norm_w, k_norm_w, ts=512):
    b, s, d = h3.shape
    n_q, n_kv = w_q.shape[1] // HEAD_DIM, w_k.shape[1] // HEAD_DIM
    q4 = HEAD_DIM // 4
    cos, sin = _rope_tables(s)
    zeros = jnp.zeros_like(sin[:, :q4])
    c64 = jnp.concatenate([cos[:, :q4], cos[:, :q4], cos[:, q4:], cos[:, q4:]], axis=1)
    sm64 = jnp.concatenate([-sin[:, :q4], zeros, -sin[:, q4:], zeros], axis=1)
    sp64 = jnp.concatenate([zeros, sin[:, :q4], zeros, sin[:, q4:]], axis=1)
    kc, ksm, ksp = (jnp.concatenate([t, t], axis=1) for t in (c64, sm64, sp64))
    head_of = jnp.arange(n_kv * HEAD_DIM) // HEAD_DIM
    bd = (head_of[:, None] == head_of[None, :]).astype(BF16)
    wqv_t = jnp.concatenate([w_q, w_v], axis=1).T.astype(BF16)
    rows = wqv_t.shape[0]
    full = lambda shape: pl.BlockSpec(shape, lambda bi, i: (0,) * len(shape))
    return pl.pallas_call(
        functools.partial(_aproj_body, n_q, n_kv),
        grid=(b, s // ts),
        in_specs=[
            pl.BlockSpec((1, ts, d), lambda bi, i: (bi, i, 0)),
            full((rows, d)), full((d, n_kv * HEAD_DIM)),
            full((HEAD_DIM, 1)), full((1, n_kv * HEAD_DIM)),
            pl.BlockSpec((2 * q4, ts), lambda bi, i: (0, i)),
            pl.BlockSpec((2 * q4, ts), lambda bi, i: (0, i)),
            pl.BlockSpec((ts, 128), lambda bi, i: (i, 0)),
            pl.BlockSpec((ts, 128), lambda bi, i: (i, 0)),
            pl.BlockSpec((ts, 128), lambda bi, i: (i, 0)),
            full((n_kv * HEAD_DIM, n_kv * HEAD_DIM)),
        ],
        out_specs=[
            pl.BlockSpec((1, n_q, HEAD_DIM, ts), lambda bi, i: (bi, 0, 0, i)),
            pl.BlockSpec((1, n_kv, ts, HEAD_DIM), lambda bi, i: (bi, 0, i, 0)),
            pl.BlockSpec((1, n_kv, HEAD_DIM + V_PAD_ROWS, ts), lambda bi, i: (bi, 0, 0, i)),
        ],
        out_shape=[
            jax.ShapeDtypeStruct((b, n_q, HEAD_DIM, s), BF16),
            jax.ShapeDtypeStruct((b, n_kv, s, HEAD_DIM), BF16),
            jax.ShapeDtypeStruct((b, n_kv, HEAD_DIM + V_PAD_ROWS, s), BF16),
        ],
        scratch_shapes=[pltpu.VMEM((rows, ts), F32)],
        compiler_params=_params("parallel", "parallel"),
        name="aproj",
    )(h3, wqv_t, w_k.astype(BF16), q_norm_w.reshape(HEAD_DIM, 1),
      jnp.tile(k_norm_w, n_kv).reshape(1, n_kv * HEAD_DIM), cos.T, sin.T, kc, ksm, ksp, bd)


def _attn_a_body(tk, qT_ref, k_ref, vT_ref, o_ref, q_scr, acc_scr, sa_scr, sb_scr, p_scr):
    s_len = k_ref.shape[2]
    tq = qT_ref.shape[3]
    w = GQA_RATIO * tq
    n_steps = s_len // tk
    rc = 32
    for hh in range(GQA_RATIO):
        q_scr[:, hh * tq:(hh + 1) * tq] = qT_ref[0, hh]
    acc_scr[...] = jnp.zeros(acc_scr.shape, F32)

    def scores(j, s_scr):
        off = pl.multiple_of(j * tk, tk)
        s_scr[...] = jnp.dot(k_ref[0, 0, pl.ds(off, tk), :], q_scr[...], preferred_element_type=F32)

    def half_step(j, m, s_scr, nxt_scr):
        scores(jnp.minimum(j + 1, n_steps - 1), nxt_scr)
        off = pl.multiple_of(j * tk, tk)
        v = vT_ref[0, 0, :, pl.ds(off, tk)]
        m8 = jnp.full((8, w), NEG_BIG, F32)
        for c in range(tk // rc):
            sc = s_scr[c * rc:(c + 1) * rc, :]
            m8 = jnp.maximum(m8, jnp.max(sc.reshape(rc // 8, 8, w), axis=0))
        m_new = jnp.maximum(m, jnp.max(m8, axis=0, keepdims=True))
        for c in range(tk // rc):
            sc = s_scr[c * rc:(c + 1) * rc, :]
            p_scr[c * rc:(c + 1) * rc, :] = jnp.exp2(sc - m_new).astype(BF16)
        alpha = jnp.exp2(m - m_new)
        acc_scr[...] = alpha * acc_scr[...] + jnp.dot(v, p_scr[...], preferred_element_type=F32)
        return m_new

    def kv_step(jj, m):
        m = half_step(2 * jj, m, sa_scr, sb_scr)
        return half_step(2 * jj + 1, m, sb_scr, sa_scr)

    scores(0, sa_scr)
    lax.fori_loop(0, n_steps // 2, kv_step, jnp.full((1, w), NEG_BIG, F32))
    o = acc_scr[:HEAD_DIM, :] / acc_scr[HEAD_DIM:HEAD_DIM + 1, :]
    for hh in range(GQA_RATIO):
        o_ref[0, :, hh * HEAD_DIM:(hh + 1) * HEAD_DIM] = o[:, hh * tq:(hh + 1) * tq].T.astype(o_ref.dtype)


def _attn_a_bounded_body(tk, qT_ref, k_ref, vT_ref, o_ref, q_scr, pa_scr, pb_scr, acc_scr):
    s_len = k_ref.shape[2]
    tq = qT_ref.shape[3]
    n_steps = s_len // tk
    for hh in range(GQA_RATIO):
        q_scr[:, hh * tq:(hh + 1) * tq] = qT_ref[0, hh]
    acc_scr[...] = jnp.zeros(acc_scr.shape, F32)

    def chunk(j):
        return pl.ds(j * tk if isinstance(j, int) else pl.multiple_of(j * tk, tk), tk)

    def probs(j, p_scr):
        s = jnp.dot(k_ref[0, 0, chunk(j), :], q_scr[...], preferred_element_type=F32)
        p_scr[...] = jnp.exp2(s).astype(BF16)

    def half_step(j, p_scr, nxt_scr):
        if nxt_scr is not None:
            probs(j + 1, nxt_scr)
        acc_scr[...] += jnp.dot(vT_ref[0, 0, :, chunk(j)], p_scr[...], preferred_element_type=F32)

    def kv_step(jj, carry):
        half_step(2 * jj, pa_scr, pb_scr)
        half_step(2 * jj + 1, pb_scr, pa_scr)
        return carry

    probs(0, pa_scr)
    lax.fori_loop(0, n_steps // 2 - 1, kv_step, 0)
    half_step(n_steps - 2, pa_scr, pb_scr)
    half_step(n_steps - 1, pb_scr, None)
    o = acc_scr[:HEAD_DIM, :] / acc_scr[HEAD_DIM:HEAD_DIM + 1, :]
    for hh in range(GQA_RATIO):
        o_ref[0, :, hh * HEAD_DIM:(hh + 1) * HEAD_DIM] = o[:, hh * tq:(hh + 1) * tq].T.astype(o_ref.dtype)


def _attn_a_call(body, tq, scratch, name, qT, k, vT):
    b, n_q, _, s = qT.shape
    n_kv, vr = k.shape[1], vT.shape[2]
    return pl.pallas_call(
        body,
        grid=(b, n_kv, s // tq),
        in_specs=[
            pl.BlockSpec((1, GQA_RATIO, HEAD_DIM, tq), lambda bi, g, i: (bi, g, 0, i)),
            pl.BlockSpec((1, 1, s, HEAD_DIM), lambda bi, g, i: (bi, g, 0, 0)),
            pl.BlockSpec((1, 1, vr, s), lambda bi, g, i: (bi, g, 0, 0)),
        ],
        out_specs=pl.BlockSpec((1, tq, GQA_RATIO * HEAD_DIM), lambda bi, g, i: (bi, i, g)),
        out_shape=jax.ShapeDtypeStruct((b, s, n_q * HEAD_DIM), BF16),
        scratch_shapes=scratch,
        compiler_params=_params("parallel", "parallel", "parallel"),
        name=name,
    )(qT, k, vT)


def _attn_a(qT, k, vT, score_bound, tq=256, tk=256, tq_bounded=1024, tk_bounded=512):
    s = qT.shape[3]
    vr = vT.shape[2]
    tq_bounded, tk_bounded = min(tq_bounded, s), min(tk_bounded, s // 2)
    assert s % (2 * tk) == 0 and s % tq == 0 and s % (2 * tk_bounded) == 0 and s % tq_bounded == 0
    w, wb = GQA_RATIO * tq, GQA_RATIO * tq_bounded
    general = functools.partial(
        _attn_a_call, functools.partial(_attn_a_body, tk), tq,
        [pltpu.VMEM((HEAD_DIM, w), BF16), pltpu.VMEM((vr, w), F32),
         pltpu.VMEM((tk, w), F32), pltpu.VMEM((tk, w), F32), pltpu.VMEM((tk, w), BF16)], "attn_a")
    bounded = functools.partial(
        _attn_a_call, functools.partial(_attn_a_bounded_body, tk_bounded), tq_bounded,
        [pltpu.VMEM((HEAD_DIM, wb), BF16), pltpu.VMEM((tk_bounded, wb), BF16),
         pltpu.VMEM((tk_bounded, wb), BF16), pltpu.VMEM((vr, wb), F32)], "attn_a_bounded")
    return lax.cond(score_bound <= MAX_SAFE_LOG2_SCORE, bounded, general, qT, k, vT)


def _bproj_body(dilation, h_ref, w_ref, o_ref, scr):
    res = jnp.dot(h_ref[...], w_ref[...], preferred_element_type=F32)
    if dilation == 1:
        o_ref[0] = res.astype(o_ref.dtype)
    else:
        rows = scr.shape[1] // dilation
        for c in range(scr.shape[0]):
            scr[c] = res[:, c * 128:(c + 1) * 128]
        for r in range(dilation):
            for c in range(scr.shape[0]):
                o_ref[r, :, c * 128:(c + 1) * 128] = scr[c, pl.ds(r, rows, stride=dilation), :].astype(o_ref.dtype)


def _bproj(h, w, batch, dilation, tm=512):
    n, d = h.shape
    e = w.shape[1]
    s = n // batch
    per_b = s // tm
    assert tm % (16 * dilation) == 0
    return pl.pallas_call(
        functools.partial(_bproj_body, dilation),
        grid=(n // tm,),
        in_specs=[pl.BlockSpec((tm, d), lambda i: (i, 0)), pl.BlockSpec((d, e), lambda i: (0, 0))],
        out_specs=pl.BlockSpec((None, dilation, tm // dilation, e),
                               lambda i: (i // per_b, 0, i % per_b, 0)),
        out_shape=jax.ShapeDtypeStruct((batch, dilation, s // dilation, e), BF16),
        scratch_shapes=[pltpu.VMEM((e // 128, tm, 128), F32)],
        compiler_params=_params("parallel"),
        name=f"bproj_d{dilation}",
    )(h, w)


def _attn_b_body(dilation, radius, q_ref, kp_ref, kc_ref, kn_ref, vp_ref, vc_ref, vn_ref,
                 o_ref, lse_ref):
    tu = q_ref.shape[0]
    n_pairs = q_ref.shape[1] // 128
    i = pl.program_id(2)
    n_u = pl.num_programs(2) * tu
    row = lax.broadcasted_iota(I32, (tu, 2 * tu), 0)
    col = lax.broadcasted_iota(I32, (tu, 2 * tu), 1)
    j = col - radius - row
    key = i * tu - radius + col
    valid = (jnp.abs(j) <= radius) & (key >= 0) & (key < n_u)
    dist = (dilation * jnp.abs(j)).astype(F32)
    pen = jnp.where(valid, 0.0, NEG_BIG)
    first = lax.broadcasted_iota(I32, (1, 128), 1) < HEAD_DIM
    scale = HEAD_DIM ** -0.5
    n_heads = 2 * n_pairs
    for hp in range(n_pairs):
        lanes = slice(hp * 128, (hp + 1) * 128)
        q2 = q_ref[:, lanes]
        kw = jnp.concatenate([kp_ref[tu - radius:, lanes], kc_ref[:, lanes],
                              kn_ref[:tu - radius, lanes]], axis=0)
        vw = jnp.concatenate([vp_ref[tu - radius:, lanes], vc_ref[:, lanes],
                              vn_ref[:tu - radius, lanes]], axis=0)
        o2 = jnp.zeros((tu, 128), F32)
        lse2 = jnp.zeros((tu, 128), F32)
        for sub in range(2):
            sel = first if sub == 0 else jnp.logical_not(first)
            slope = 2.0 ** (-8.0 * (2 * hp + sub + 1) / n_heads)
            qm = jnp.where(sel, q2, jnp.zeros_like(q2))
            s = lax.dot_general(qm, kw, (((1,), (1,)), ((), ())), preferred_element_type=F32)
            s = s * scale + (pen - slope * dist)
            m = jnp.max(s, axis=1, keepdims=True)
            p = jnp.exp(s - m)
            l = jnp.sum(p, axis=1, keepdims=True)
            vm = jnp.where(sel, vw, jnp.zeros_like(vw))
            o2 = o2 + jnp.dot(p.astype(BF16), vm, preferred_element_type=F32) / l
            lse2 = jnp.where(sel, m + jnp.log(l), lse2)
        o_ref[:, lanes] = o2.astype(o_ref.dtype)
        lse_ref[:, lanes] = lse2


def _attn_b(pbd, window, tu=128):
    b, dilation, u, e = pbd.shape
    width = e // 3
    radius = window // (2 * dilation)
    assert radius < tu and u % tu == 0
    n_t = u // tu

    def spec(which, shift):
        def index(bi, r, i):
            return (bi, r, jnp.clip(i + shift, 0, n_t - 1), which)
        return pl.BlockSpec((None, None, tu, width), index)

    out_spec = pl.BlockSpec((None, None, tu, width), lambda bi, r, i: (bi, r, i, 0))
    return pl.pallas_call(
        functools.partial(_attn_b_body, dilation, radius),
        grid=(b, dilation, n_t),
        in_specs=[spec(0, 0), spec(1, -1), spec(1, 0), spec(1, 1), spec(2, -1), spec(2, 0), spec(2, 1)],
        out_specs=[out_spec, out_spec],
        out_shape=[jax.ShapeDtypeStruct((b, dilation, u, width), BF16),
                   jax.ShapeDtypeStruct((b, dilation, u, width), F32)],
        compiler_params=_params("parallel", "parallel", "parallel"),
        name=f"attn_b_d{dilation}",
    )(pbd, pbd, pbd, pbd, pbd, pbd, pbd)


def _rms(x, w):
    ms = jnp.mean(x * x, axis=-1, keepdims=True)
    return x * lax.rsqrt(ms + NORM_EPS) * w


def _to_slabs(ref, x):
    rows, width = x.shape
    c = width // 128
    for j in range(c):
        ref[pl.ds(j, rows, stride=c), :] = x[:, j * 128:(j + 1) * 128]


def _from_slabs(ref, rows, c):
    return jnp.concatenate([ref[pl.ds(j, rows, stride=c), :] for j in range(c)], axis=1)


def _token_order(ref, scr):
    d, rows, _ = ref.shape
    if d == 1:
        return ref[0].astype(F32)
    for r in range(d):
        slab = ref[r].astype(F32)
        for c in range(scr.shape[0]):
            scr[c, pl.ds(r, rows, stride=d), :] = slab[:, c * 128:(c + 1) * 128]
    return jnp.concatenate([scr[c] for c in range(scr.shape[0])], axis=1)


def _outproj_body(n_groups, *refs):
    oa_ref = refs[0]
    ob_refs = refs[1:1 + n_groups]
    lse_refs = refs[1 + n_groups:1 + 2 * n_groups]
    (x_ref, wo_a_ref, wo_b_ref, nwa_ref, nwb_ref, nm_ref, wr_ref, br_ref,
     x1_ref, hm_ref, lg_ref) = refs[1 + 2 * n_groups:12 + 2 * n_groups]
    scrs = refs[12 + 2 * n_groups:]
    obs = [_token_order(r, scrs[2 * g]) for g, r in enumerate(ob_refs)]
    lses = [_token_order(r, scrs[2 * g + 1]) for g, r in enumerate(lse_refs)]
    mx = functools.reduce(jnp.maximum, lses)
    ws = [jnp.exp(v - mx) for v in lses]
    den = functools.reduce(lambda a, c: a + c, ws)
    ob = functools.reduce(lambda a, c: a + c, [w * o for w, o in zip(ws, obs)]) / den
    na = _rms(oa_ref[...].astype(F32), nwa_ref[...]).astype(BF16)
    nb = _rms(ob, nwb_ref[...]).astype(BF16)
    x1 = (x_ref[...] + jnp.dot(na, wo_a_ref[...], preferred_element_type=F32)
          + jnp.dot(nb, wo_b_ref[...], preferred_element_type=F32))
    _to_slabs(x1_ref, x1)
    hm = _rms(x1, nm_ref[...])
    hm_hi = hm.astype(BF16)
    half = hm.shape[1] // 2
    bits = lax.bitcast_convert_type(hm_hi.astype(F32), jnp.uint32)
    _to_slabs(hm_ref, bits[:, :half] | (bits[:, half:] >> 16))
    hm_lo = (hm - hm_hi.astype(F32)).astype(BF16)
    lg_ref[...] = (jnp.dot(hm_hi, wr_ref[0], preferred_element_type=F32)
                   + jnp.dot(hm_lo, wr_ref[0], preferred_element_type=F32)
                   + jnp.dot(hm_hi, wr_ref[1], preferred_element_type=F32)) + br_ref[...]


def _outproj(o_a, o_bs, lses, x2, w_out, out_norm_w, norm_moe_w, w_router, b_router, tm=256):
    n, d = x2.shape
    wa, wb = o_a.shape[1], o_bs[0].shape[3]
    n_e = w_router.shape[1]
    per_b = n // o_bs[0].shape[0] // tm
    row = lambda width: pl.BlockSpec((tm, width), lambda i: (i, 0))
    full = lambda shape: pl.BlockSpec(shape, lambda i: (0,) * len(shape))
    grouped = [pl.BlockSpec((None, o.shape[1], tm // o.shape[1], wb),
                            lambda i: (i // per_b, 0, i % per_b, 0)) for o in o_bs]
    n_groups = len(o_bs)
    c = d // 128
    slabs = pl.BlockSpec((tm * c, 128), lambda i: (i, 0))
    wr_hi = w_router.astype(BF16)
    return pl.pallas_call(
        functools.partial(_outproj_body, n_groups),
        grid=(n // tm,),
        in_specs=[row(wa)] + grouped + grouped + [
            row(d), full((wa, d)), full((wb, d)), full((1, wa)), full((1, wb)), full((1, d)),
            full((2, d, n_e)), full((1, n_e))],
        out_specs=[slabs, pl.BlockSpec((tm * c // 2, 128), lambda i: (i, 0)), row(n_e)],
        out_shape=[jax.ShapeDtypeStruct((n * c, 128), F32), jax.ShapeDtypeStruct((n * c // 2, 128), jnp.uint32),
                   jax.ShapeDtypeStruct((n, n_e), F32)],
        scratch_shapes=[pltpu.VMEM((wb // 128, tm, 128), F32)] * (2 * n_groups),
        compiler_params=_params("parallel"),
        name="outproj",
    )(o_a, *o_bs, *lses, x2, w_out[:wa].astype(BF16), w_out[wa:].astype(BF16),
      out_norm_w[:wa].reshape(1, wa), out_norm_w[wa:].reshape(1, wb), norm_moe_w.reshape(1, d),
      jnp.stack([wr_hi, (w_router - wr_hi.astype(F32)).astype(BF16)]), b_router.reshape(1, n_e))


def _router_body(lg_ref, idx_ref, gate_ref, rank_ref, cnt_ref, carry_ref):
    i = pl.program_id(0)
    tm, n_e = lg_ref.shape

    @pl.when(i == 0)
    def _():
        carry_ref[...] = jnp.zeros_like(carry_ref)

    lane = lax.broadcasted_iota(I32, (tm, n_e), 1)
    work = lg_ref[...]
    vals, idxs = [], []
    for _ in range(TOP_K):
        mx = jnp.max(work, axis=1, keepdims=True)
        ix = jnp.min(jnp.where(work == mx, lane, n_e), axis=1, keepdims=True)
        vals.append(mx)
        idxs.append(ix)
        work = jnp.where(lane == ix, -jnp.inf, work)
    es = [jnp.exp(v - vals[0]) for v in vals]
    den = functools.reduce(lambda a, c: a + c, es)
    chosen = functools.reduce(lambda a, c: a + c, [(lane == ix).astype(F32) for ix in idxs])
    r = lax.broadcasted_iota(I32, (tm, tm), 0)
    c = lax.broadcasted_iota(I32, (tm, tm), 1)
    before = jnp.where(c < r, 1.0, 0.0).astype(BF16)
    prefix = jnp.dot(before, chosen.astype(BF16), preferred_element_type=F32) + carry_ref[...]
    for k in range(TOP_K):
        idx_ref[:, k:k + 1] = idxs[k]
        gate_ref[:, k:k + 1] = es[k] / den
        rank_ref[:, k:k + 1] = jnp.sum(jnp.where(lane == idxs[k], prefix, 0.0), axis=1,
                                       keepdims=True).astype(I32)
    carry_ref[...] += jnp.sum(chosen, axis=0, keepdims=True)
    cnt_ref[...] = carry_ref[...].astype(I32)


def _router(logits, tm=256):
    n, n_e = logits.shape
    row = pl.BlockSpec((tm, TOP_K), lambda i: (i, 0))
    return pl.pallas_call(
        _router_body,
        grid=(n // tm,),
        in_specs=[pl.BlockSpec((tm, n_e), lambda i: (i, 0))],
        out_specs=[row, row, row, pl.BlockSpec((1, n_e), lambda i: (0, 0))],
        out_shape=[jax.ShapeDtypeStruct((n, TOP_K), I32), jax.ShapeDtypeStruct((n, TOP_K), F32),
                   jax.ShapeDtypeStruct((n, TOP_K), I32), jax.ShapeDtypeStruct((1, n_e), I32)],
        scratch_shapes=[pltpu.VMEM((1, n_e), F32)],
        compiler_params=_params("arbitrary"),
        name="router",
    )(logits)


def _slab_copy(src_ref, row, buf_ref, slot, c, sem):
    return pltpu.make_async_copy(src_ref.at[pl.ds(pl.multiple_of(row * c, c), c)],
                                 buf_ref.at[pl.ds(pl.multiple_of(slot * c, c), c)], sem)


def _two_slot_steps(issue, consume):
    i = pl.program_id(0)

    @pl.when(i == 0)
    def _():
        issue(0, 0)

    for slot in range(2):
        @pl.when(i % 2 == slot)
        def _():
            @pl.when(i + 1 < pl.num_programs(0))
            def _():
                issue(1, 1 - slot)

            consume(slot)


def _row_gather_body(rows_per_step, c, idx_ref, nxt_ref, src_ref, dst_ref, buf_ref, sem):
    def issue(step_offset, slot):
        ref = nxt_ref if step_offset else idx_ref

        def one(r, carry):
            _slab_copy(src_ref, ref[0, 0, r], buf_ref.at[slot], r, c, sem.at[slot]).start()
            return carry

        lax.fori_loop(0, rows_per_step, one, 0, unroll=8)

    def consume(slot):
        pltpu.make_async_copy(src_ref.at[pl.ds(0, rows_per_step * c)], buf_ref.at[slot], sem.at[slot]).wait()
        bits = _from_slabs(buf_ref.at[slot], rows_per_step, c)
        half = c * 128
        dst_ref[:, :half] = lax.bitcast_convert_type(bits & jnp.uint32(0xFFFF0000), F32).astype(dst_ref.dtype)
        dst_ref[:, half:] = lax.bitcast_convert_type(bits << 16, F32).astype(dst_ref.dtype)

    _two_slot_steps(issue, consume)


def _row_gather(src, idx, c, rows_per_step=2048):
    n_out = idx.shape[0]
    while n_out % rows_per_step:
        rows_per_step //= 2
    steps = n_out // rows_per_step
    idx3 = idx.reshape(steps, 1, rows_per_step)
    return pl.pallas_call(
        functools.partial(_row_gather_body, rows_per_step, c),
        grid=(steps,),
        in_specs=[pl.BlockSpec((1, 1, rows_per_step), lambda i: (i, 0, 0), memory_space=pltpu.SMEM),
                  pl.BlockSpec((1, 1, rows_per_step), lambda i: (jnp.minimum(i + 1, steps - 1), 0, 0),
                               memory_space=pltpu.SMEM),
                  pl.BlockSpec(memory_space=pl.ANY)],
        out_specs=pl.BlockSpec((rows_per_step, 2 * c * 128), lambda i: (i, 0)),
        out_shape=jax.ShapeDtypeStruct((n_out, 2 * c * 128), BF16),
        scratch_shapes=[pltpu.VMEM((2, rows_per_step * c, 128), src.dtype), pltpu.SemaphoreType.DMA((2,))],
        compiler_params=_params("arbitrary"),
        name="dispatch",
    )(idx3, idx3, src)


def _ffn_body(blk_e_ref, n_used_ref, x_ref, wg_ref, wu_ref, bg_ref, bu_ref, wd_ref, bd_ref, g_ref,
              y_ref, act_ref):
    i, f = pl.program_id(0), pl.program_id(1)
    n_f = pl.num_programs(1)
    used = i < n_used_ref[0]

    @pl.when(used)
    def _():
        x = x_ref[...]
        tf = wg_ref.shape[2]
        gate = jnp.dot(x, wg_ref[0], preferred_element_type=F32) + bg_ref[0]
        up = jnp.dot(x, wu_ref[0], preferred_element_type=F32) + bu_ref[0]
        gate = jnp.minimum(gate, SWIGLU_LIMIT)
        up = jnp.clip(up, -SWIGLU_LIMIT, SWIGLU_LIMIT)
        act = (up + 1.0) * gate * jax.nn.sigmoid(SWIGLU_ALPHA * gate)
        act_ref[:, pl.ds(pl.multiple_of(f * tf, tf), tf)] = act.astype(BF16)

        @pl.when(f == n_f - 1)
        def _():
            y = jnp.dot(act_ref[...], wd_ref[0], preferred_element_type=F32)
            _to_slabs(y_ref, (y + bd_ref[0]) * g_ref[...])

    @pl.when(jnp.logical_not(used) & (f == n_f - 1))
    def _():
        y_ref[...] = jnp.zeros_like(y_ref)


def _expert_ffn(xs, blk_e, n_used, w_gu, b_gu, w_down, b_down, gate_buf, tf=512):
    cap, d = xs.shape
    n_e, _, two_f = w_gu.shape
    ff = two_f // 2
    n_f = ff // tf
    n_blocks = cap // MOE_BLOCK

    def fidx(i, f, nu):
        return jnp.where(i < nu[0], f, n_f - 1)

    grid_spec = pltpu.PrefetchScalarGridSpec(
        num_scalar_prefetch=2,
        grid=(n_blocks, n_f),
        in_specs=[
            pl.BlockSpec((MOE_BLOCK, d), lambda i, f, be, nu: (jnp.minimum(i, nu[0] - 1), 0)),
            pl.BlockSpec((1, d, tf), lambda i, f, be, nu: (be[i], 0, fidx(i, f, nu))),
            pl.BlockSpec((1, d, tf), lambda i, f, be, nu: (be[i], 0, n_f + fidx(i, f, nu))),
            pl.BlockSpec((1, 1, tf), lambda i, f, be, nu: (be[i], 0, fidx(i, f, nu))),
            pl.BlockSpec((1, 1, tf), lambda i, f, be, nu: (be[i], 0, n_f + fidx(i, f, nu))),
            pl.BlockSpec((1, ff, d), lambda i, f, be, nu: (be[i], 0, 0)),
            pl.BlockSpec((1, 1, d), lambda i, f, be, nu: (be[i], 0, 0)),
            pl.BlockSpec((MOE_BLOCK, 1), lambda i, f, be, nu: (i, 0)),
        ],
        out_specs=pl.BlockSpec((MOE_BLOCK * (d // 128), 128), lambda i, f, be, nu: (i, 0)),
        scratch_shapes=[pltpu.VMEM((MOE_BLOCK, ff), BF16)],
    )
    return pl.pallas_call(
        _ffn_body,
        grid_spec=grid_spec,
        out_shape=jax.ShapeDtypeStruct((cap * (d // 128), 128), F32),
        compiler_params=_params("arbitrary", "arbitrary"),
        name="expert_ffn",
    )(blk_e, n_used, xs, w_gu, w_gu, b_gu.reshape(n_e, 1, two_f), b_gu.reshape(n_e, 1, two_f),
      w_down, b_down.reshape(n_e, 1, d), gate_buf.reshape(cap, 1))


def _combine_body(tm, c, dest_ref, nxt_ref, y_ref, x1_ref, w_ref, o_ref, buf_ref, sum_ref, sem):
    def issue(step_offset, slot):
        ref = nxt_ref if step_offset else dest_ref

        def one(t, carry):
            for k in range(TOP_K):
                _slab_copy(y_ref, ref[0, 0, t * TOP_K + k], buf_ref.at[slot, k], t, c, sem.at[slot]).start()
            return carry

        lax.fori_loop(0, tm, one, 0, unroll=2)

    def consume(slot):
        for k in range(TOP_K):
            pltpu.make_async_copy(y_ref.at[pl.ds(0, tm * c)], buf_ref.at[slot, k], sem.at[slot]).wait()
        x = x1_ref[...]
        for k in range(TOP_K):
            x = x + buf_ref[slot, k]
        x = x.reshape(tm, c, 128)
        ms = jnp.sum(jnp.sum(x * x, axis=2, keepdims=True), axis=1, keepdims=True) / (c * 128)
        sum_ref[...] = (x * lax.rsqrt(ms + NORM_EPS)).reshape(tm * c, 128)
        o_ref[...] = _from_slabs(sum_ref, tm, c) * w_ref[...]

    _two_slot_steps(issue, consume)


def _combine(y, dest, x1, w, c, tm=256):
    n = x1.shape[0] // c
    steps = n // tm
    dest3 = dest.reshape(steps, 1, tm * TOP_K)
    return pl.pallas_call(
        functools.partial(_combine_body, tm, c),
        grid=(steps,),
        in_specs=[pl.BlockSpec((1, 1, tm * TOP_K), lambda i: (i, 0, 0), memory_space=pltpu.SMEM),
                  pl.BlockSpec((1, 1, tm * TOP_K), lambda i: (jnp.minimum(i + 1, steps - 1), 0, 0),
                               memory_space=pltpu.SMEM),
                  pl.BlockSpec(memory_space=pl.ANY),
                  pl.BlockSpec((tm * c, 128), lambda i: (i, 0)),
                  pl.BlockSpec((1, c * 128), lambda i: (0, 0))],
        out_specs=pl.BlockSpec((tm, c * 128), lambda i: (i, 0)),
        out_shape=jax.ShapeDtypeStruct((n, c * 128), F32),
        scratch_shapes=[pltpu.VMEM((2, TOP_K, tm * c, 128), F32), pltpu.VMEM((tm * c, 128), F32),
                        pltpu.SemaphoreType.DMA((2,))],
        compiler_params=_params("arbitrary"),
        name="combine",
    )(dest3, dest3, y, x1, w.reshape(1, c * 128))


def _layer(x, norm_mix_w, w_in, q_norm_w, k_norm_w, out_norm_w, w_out, norm_moe_w,
           w_router, b_router, w_gate_up, b_gate_up, w_down, b_down):
    b, s, d = x.shape
    n = b * s
    x2 = x.reshape(n, d)
    b_w = d // 4
    a_q_w = d - b_w
    a_kv_w = a_q_w // GQA_RATIO

    h = _prenorm(x2, norm_mix_w)
    qT, k, vT = _aproj(h.reshape(b, s, d), w_in[:, :a_q_w], w_in[:, a_q_w:a_q_w + a_kv_w],
                       w_in[:, a_q_w + a_kv_w:a_q_w + 2 * a_kv_w], q_norm_w, k_norm_w)
    score_bound = (1.02 * HEAD_DIM ** 0.5 * LOG2E) * jnp.max(jnp.abs(q_norm_w)) * jnp.max(jnp.abs(k_norm_w))
    o_a = _attn_a(qT, k, vT, score_bound).reshape(n, a_q_w)

    o_bs, lses = [], []
    base = a_q_w + 2 * a_kv_w
    for g, (window, dilation) in enumerate(B_CONFIGS):
        w_g = w_in[:, base + 3 * g * b_w:base + 3 * (g + 1) * b_w].astype(BF16)
        o_g, lse_g = _attn_b(_bproj(h, w_g, b, dilation), window)
        o_bs.append(o_g)
        lses.append(lse_g)

    x1, hm, logits = _outproj(o_a, o_bs, lses, x2, w_out, out_norm_w, norm_moe_w, w_router, b_router)

    top_idx, gates, rank, counts = _router(logits)
    counts = counts.reshape(N_EXPERTS)
    padded = (counts + MOE_BLOCK - 1) // MOE_BLOCK * MOE_BLOCK
    pend = jnp.cumsum(padded)
    pstart = pend - padded
    dest = pstart[top_idx] + rank
    n_blocks = -(-(n * TOP_K) // MOE_BLOCK) + N_EXPERTS
    cap = n_blocks * MOE_BLOCK
    n_assign = n * TOP_K
    assign = jnp.full((cap,), n_assign, I32).at[dest.reshape(-1)].set(jnp.arange(n_assign, dtype=I32))
    tok_buf = jnp.minimum(assign // TOP_K, n - 1)
    gate_buf = jnp.concatenate([gates.reshape(-1), jnp.zeros((1,), F32)])[assign]
    blk_start = jnp.arange(n_blocks, dtype=I32) * MOE_BLOCK
    blk_e = jnp.minimum(jnp.sum((pend[None, :] <= blk_start[:, None]).astype(I32), axis=1), N_EXPERTS - 1)
    n_used = (pend[-1] // MOE_BLOCK).astype(I32).reshape(1)

    xs = _row_gather(hm, tok_buf, d // 256)
    y = _expert_ffn(xs, blk_e, n_used, w_gate_up.astype(BF16), b_gate_up, w_down.astype(BF16), b_down, gate_buf)
    return y, dest, x1


def kernel(x, norm_mix_w, w_in, q_norm_w, k_norm_w, out_norm_w, w_out, norm_moe_w, w_router, b_router,
           w_gate_up, b_gate_up, w_down, b_down, final_norm_w):
    b, s, d = x.shape
    assert w_in.shape[0] == 1, "the MoE combine is fused with the final norm: single-layer stacks only"
    y, dest, x1 = _layer(x, norm_mix_w[0], w_in[0], q_norm_w[0], k_norm_w[0], out_norm_w[0], w_out[0],
                         norm_moe_w[0], w_router[0], b_router[0], w_gate_up[0], b_gate_up[0],
                         w_down[0], b_down[0])
    return _combine(y, dest, x1, final_norm_w, d // 128).reshape(b, s, d)
```
